```python
import math
import jax, jax.numpy as jnp
from jax import lax
import numpy as np

D_MODEL = 1024
BATCH = 4
SEQ = 4096
DEPTH = 4

CHUNK = 64
Q_BLOCK = 128
N_MIXERS = 3
D_FF = 4 * D_MODEL
EPS = 1e-6

A_HEADS = 8
A_HEAD_DIM = D_MODEL // (2 * A_HEADS)
A_V_DIM = 2 * A_HEAD_DIM

POOL_WINDOWS = (2, 4, 8, 16)
POOL_GROUP = D_MODEL // len(POOL_WINDOWS)

C_HEADS = 8
C_HEAD_DIM = D_MODEL // C_HEADS
C_KV_HEADS = 2
IDX_HEADS = 8
IDX_DIM = 64
TOPK_MAX = 256
C_SPLIT_SIZES = (C_HEADS * C_HEAD_DIM, C_KV_HEADS * C_HEAD_DIM, C_KV_HEADS * C_HEAD_DIM,
                 IDX_HEADS * IDX_DIM, IDX_DIM, IDX_HEADS)
C_IN = sum(C_SPLIT_SIZES)

kernel_name = "hybrid_diffattn_pool_dsa_trunk"


def _n_layers_of(mixer):
    return len([i for i in range(DEPTH) if i % N_MIXERS == mixer])


def rms_norm(x, g):
    xf = x.astype(jnp.float32)
    y = xf * lax.rsqrt(jnp.mean(xf * xf, axis=-1, keepdims=True) + EPS)
    return (y * g.astype(jnp.float32)).astype(x.dtype)


def diff_attention(h, w_in, q_g, k_g, lq1, lk1, lq2, lk2, sub_g, w_out, lambda_init):
    B, S, _ = h.shape
    H, Dh = A_HEADS, A_HEAD_DIM
    q, k, v = jnp.split(h @ w_in, 3, axis=-1)
    q = rms_norm(q.reshape(B, S, H, 2, Dh), q_g)
    k = rms_norm(k.reshape(B, S, H, 2, Dh), k_g)
    v = v.reshape(B, S, H, A_V_DIM)
    f32 = jnp.float32
    lam = (jnp.exp(jnp.sum(lq1.astype(f32) * lk1.astype(f32)))
           - jnp.exp(jnp.sum(lq2.astype(f32) * lk2.astype(f32))) + lambda_init)
    nb = S // Q_BLOCK
    qb = q.reshape(B, nb, Q_BLOCK, H, 2, Dh).transpose(1, 0, 2, 3, 4, 5)
    key_chunk = jnp.arange(S) // CHUNK
    scale = Dh ** -0.5

    def block(args):
        qi, bi = args
        q_chunk = (bi * Q_BLOCK + jnp.arange(Q_BLOCK)) // CHUNK
        mask = key_chunk[None, :] <= q_chunk[:, None]
        s = jnp.einsum('bqhmd,bshmd->bhmqs', qi, k).astype(f32) * scale
        s = jnp.where(mask, s, -jnp.inf)
        p = jax.nn.softmax(s, axis=-1)
        pd = (p[:, :, 0] - lam * p[:, :, 1]).astype(v.dtype)
        return jnp.einsum('bhqs,bshe->bqhe', pd, v)

    o = lax.map(block, (qb, jnp.arange(nb)))
    o = o.transpose(1, 0, 2, 3, 4).reshape(B, S, H, A_V_DIM)
    o = rms_norm(o, sub_g) * (1.0 - lambda_init)
    return o.reshape(B, S, H * A_V_DIM) @ w_out


def pool_mixer(h, w_group, scale):
    B, S, D = h.shape
    hf = h.astype(jnp.float32)
    cs = jnp.pad(jnp.cumsum(hf, axis=1), ((0, 0), (1, 0), (0, 0)))
    t = jnp.arange(S)
    outs = []
    for g, w in enumerate(POOL_WINDOWS):
        c = cs[:, :, g * POOL_GROUP:(g + 1) * POOL_GROUP]
        prev = jnp.pad(c, ((0, 0), (w - 1, 0), (0, 0)))[:, :S]
        count = jnp.minimum(t + 1, w).astype(jnp.float32)
        mean = (c[:, 1:] - prev) / count[None, :, None]
        outs.append(mean - hf[:, :, g * POOL_GROUP:(g + 1) * POOL_GROUP])
    pooled = jnp.stack(outs, axis=2).astype(h.dtype)
    y = jnp.einsum('bsgc,gce->bsge', pooled, w_group).reshape(B, S, D)
    return y * scale


def dsa_attention(h, w_in, q_g, k_g, w_out):
    B, S, _ = h.shape
    H, G, Dh = C_HEADS, C_KV_HEADS, C_HEAD_DIM
    R = H // G
    f32 = jnp.float32
    offsets = list(np.cumsum(C_SPLIT_SIZES)[:-1])
    q, k, v, iq, ik, iw = jnp.split(h @ w_in, offsets, axis=-1)
    q = rms_norm(q.reshape(B, S, G, R, Dh), q_g)
    k = rms_norm(k.reshape(B, S, G, Dh), k_g)
    v = v.reshape(B, S, G, Dh)
    iq = iq.reshape(B, S, IDX_HEADS, IDX_DIM)
    iw = iw * (IDX_HEADS ** -0.5)
    topk = min(TOPK_MAX, S // 4)
    nb = S // Q_BLOCK
    qb = q.reshape(B, nb, Q_BLOCK, G, R, Dh).transpose(1, 0, 2, 3, 4, 5)
    iqb = iq.reshape(B, nb, Q_BLOCK, IDX_HEADS, IDX_DIM).transpose(1, 0, 2, 3, 4)
    iwb = iw.reshape(B, nb, Q_BLOCK, IDX_HEADS).transpose(1, 0, 2, 3)
    key_chunk = jnp.arange(S) // CHUNK

    def block(args):
        qi, iqi, iwi, bi = args
        q_chunk = (bi * Q_BLOCK + jnp.arange(Q_BLOCK)) // CHUNK
        admissible = key_chunk[None, :] <= q_chunk[:, None]
        logits = jnp.einsum('bqhd,bsd->bqhs', iqi, ik).astype(f32) * (IDX_DIM ** -0.5)
        score = jnp.einsum('bqh,bqhs->bqs', iwi.astype(f32), jax.nn.relu(logits))
        score = jnp.where(admissible[None], score, -jnp.inf)
        _, idx = lax.top_k(score, topk)
        valid = (idx // CHUNK) <= q_chunk[None, :, None]
        k_sel = jax.vmap(lambda kk, ii: kk[ii])(k, idx)
        v_sel = jax.vmap(lambda vv, ii: vv[ii])(v, idx)
        s = jnp.einsum('bqgrd,bqjgd->bqgrj', qi, k_sel).astype(f32) * (Dh ** -0.5)
        s = jnp.where(valid[:, :, None, None, :], s, -jnp.inf)
        p = jax.nn.softmax(s, axis=-1).astype(v.dtype)
        return jnp.einsum('bqgrj,bqjgd->bqgrd', p, v_sel)

    o = lax.map(block, (qb, iqb, iwb, jnp.arange(nb)))
    o = o.transpose(1, 0, 2, 3, 4, 5).reshape(B, S, H * Dh)
    return o @ w_out


def setup_inputs(seed: int = 0) -> dict:
    key = jax.random.key(seed)
    ks = jax.random.split(key, 24)
    f32 = jnp.float32
    nA, nB, nC = _n_layers_of(0), _n_layers_of(1), _n_layers_of(2)

    def w(k, shape, fan_in, gain=1.0):
        return jax.random.normal(k, shape, f32) * (gain * fan_in ** -0.5)

    def g(k, shape):
        return 1.0 + 0.02 * jax.random.normal(k, shape, f32)

    def lam(k, shape):
        return 0.1 * jax.random.normal(k, shape, f32)

    return {
        "x": jax.random.normal(ks[0], (BATCH, SEQ, D_MODEL), f32),
        "norm1_g": g(ks[1], (DEPTH, D_MODEL)),
        "norm2_g": g(ks[2], (DEPTH, D_MODEL)),
        "a_w_in": w(ks[3], (nA, D_MODEL, 3 * D_MODEL), D_MODEL),
        "a_q_norm_g": g(ks[4], (nA, A_HEAD_DIM)),
        "a_k_norm_g": g(ks[5], (nA, A_HEAD_DIM)),
        "a_lambda_q1": lam(ks[6], (nA, A_HEAD_DIM)),
        "a_lambda_k1": lam(ks[7], (nA, A_HEAD_DIM)),
        "a_lambda_q2": lam(ks[8], (nA, A_HEAD_DIM)),
        "a_lambda_k2": lam(ks[9], (nA, A_HEAD_DIM)),
        "a_subln_g": g(ks[10], (nA, A_V_DIM)),
        "a_w_out": w(ks[11], (nA, A_HEADS * A_V_DIM, D_MODEL), A_HEADS * A_V_DIM, 0.5),
        "b_w_group": w(ks[12], (nB, len(POOL_WINDOWS), POOL_GROUP, POOL_GROUP), POOL_GROUP),
        "b_scale": g(ks[13], (nB, D_MODEL)),
        "c_w_in": w(ks[14], (nC, D_MODEL, C_IN), D_MODEL),
        "c_q_norm_g": g(ks[15], (nC, C_HEAD_DIM)),
        "c_k_norm_g": g(ks[16], (nC, C_HEAD_DIM)),
        "c_w_out": w(ks[17], (nC, C_HEADS * C_HEAD_DIM, D_MODEL), C_HEADS * C_HEAD_DIM, 0.5),
        "mlp_w1": w(ks[18], (DEPTH, D_MODEL, D_FF), D_MODEL),
        "mlp_w2": w(ks[19], (DEPTH, D_FF, D_MODEL), D_FF, 0.5),
    }


def reference(x, norm1_g, norm2_g, a_w_in, a_q_norm_g, a_k_norm_g, a_lambda_q1, a_lambda_k1,
              a_lambda_q2, a_lambda_k2, a_subln_g, a_w_out, b_w_group, b_scale, c_w_in,
              c_q_norm_g, c_k_norm_g, c_w_out, mlp_w1, mlp_w2):
    h = x
    for i in range(DEPTH):
        m, j = i % N_MIXERS, i // N_MIXERS
        u = rms_norm(h, norm1_g[i])
        if m == 0:
            lambda_init = 0.8 - 0.6 * math.exp(-0.3 * i)
            y = diff_attention(u, a_w_in[j], a_q_norm_g[j], a_k_norm_g[j], a_lambda_q1[j],
                               a_lambda_k1[j], a_lambda_q2[j], a_lambda_k2[j], a_subln_g[j],
                               a_w_out[j], lambda_init)
        elif m == 1:
            y = pool_mixer(u, b_w_group[j], b_scale[j])
        else:
            y = dsa_attention(u, c_w_in[j], c_q_norm_g[j], c_k_norm_g[j], c_w_out[j])
        h = h + y
        u = rms_norm(h, norm2_g[i])
        h = h + jnp.square(jax.nn.relu(u @ mlp_w1[i])) @ mlp_w2[i]
    return h
```

```python
import functools
import math

import jax
import jax.numpy as jnp
from jax import lax
from jax.experimental import pallas as pl
from jax.experimental.pallas import tpu as pltpu

F32 = jnp.float32
BF16 = jnp.bfloat16

D_MODEL = 1024
N_MIXERS = 3
CHUNK = 64
CHUNK_SHIFT = 6
EPS = 1e-6
LANES = 128
MXU_DIM = 256

A_HEADS = 8
A_HEAD_DIM = 64
A_V_DIM = 128
POOL_WINDOWS = (2, 4, 8, 16)
POOL_GROUP = 256
POOL_HALO = 16
C_HEADS = 8
C_HEAD_DIM = 128
C_KV_HEADS = 2
IDX_HEADS = 8
IDX_DIM = 64
TOPK_MAX = 256

C_Q0, C_K0, C_V0, C_IQ0, C_IK0, C_IW0, C_COLS = 0, 1024, 1280, 1536, 2048, 2176, 2304

NEG = -1e30
VMEM_LIMIT = 56 * 1024 * 1024


def _cparams(sem):
    return pltpu.CompilerParams(dimension_semantics=sem, vmem_limit_bytes=VMEM_LIMIT)


def _rms_rows(x, g):
    ms = jnp.mean(x * x, axis=-1, keepdims=True)
    return (x * lax.rsqrt(ms + EPS)) * g


def _dot_nt(a, b):
    return lax.dot_general(a, b, (((1,), (1,)), ((), ())), preferred_element_type=F32)


def _proj_kernel(h_ref, g_ref, w_ref, gain_ref, seg_ref, o_ref, *rest, n_norm_tiles, seg, tn, aux):
    if aux:
        aux_ref, u_ref = rest
    else:
        (u_ref,) = rest
    j = pl.program_id(1)

    @pl.when(j == 0)
    def _():
        u_ref[...] = _rms_rows(h_ref[...], g_ref[...]).astype(BF16)

    acc = jnp.dot(u_ref[...], w_ref[...], preferred_element_type=F32)

    @pl.when(j < n_norm_tiles)
    def _():
        for c in range(tn // MXU_DIM):
            a = acc[:, c * MXU_DIM:(c + 1) * MXU_DIM]
            sq = a * a
            hi = sq.astype(BF16)
            lo = (sq - hi.astype(F32)).astype(BF16)
            ssum = (jnp.dot(hi, seg_ref[...], preferred_element_type=F32)
                    + jnp.dot(lo, seg_ref[...], preferred_element_type=F32))
            y = (a * lax.rsqrt(ssum * (1.0 / seg) + EPS)) * gain_ref[:, c * MXU_DIM:(c + 1) * MXU_DIM]
            o_ref[:, c * MXU_DIM:(c + 1) * MXU_DIM] = y.astype(BF16)

    @pl.when(j >= n_norm_tiles)
    def _():
        o_ref[...] = acc.astype(BF16)

    if aux:
        @pl.when(j == pl.num_programs(1) - 1)
        def _():
            aux_ref[...] = acc[:, tn - LANES:]


def _proj(h, g, w, gain, *, n_norm_cols, seg, aux):
    t, d = h.shape
    n = w.shape[1]
    tm = min(1024, t)
    tn = 512 if n % 512 == 0 else 256
    assert n % tn == 0 and n_norm_cols % tn == 0 and t % tm == 0
    r = lax.broadcasted_iota(jnp.int32, (MXU_DIM, MXU_DIM), 0) // seg
    c = lax.broadcasted_iota(jnp.int32, (MXU_DIM, MXU_DIM), 1) // seg
    segm = (r == c).astype(BF16)
    out_shape = [jax.ShapeDtypeStruct((t, n), BF16)]
    out_specs = [pl.BlockSpec((tm, tn), lambda i, j: (i, j))]
    if aux:
        out_shape.append(jax.ShapeDtypeStruct((t, LANES), F32))
        out_specs.append(pl.BlockSpec((tm, LANES), lambda i, j: (i, 0)))
    res = pl.pallas_call(
        functools.partial(_proj_kernel, n_norm_tiles=n_norm_cols // tn, seg=seg, tn=tn, aux=aux),
        grid=(t // tm, n // tn),
        in_specs=[
            pl.BlockSpec((tm, d), lambda i, j: (i, 0)),
            pl.BlockSpec((1, d), lambda i, j: (0, 0)),
            pl.BlockSpec((d, tn), lambda i, j: (0, j)),
            pl.BlockSpec((1, tn), lambda i, j: (0, j)),
            pl.BlockSpec((MXU_DIM, MXU_DIM), lambda i, j: (0, 0)),
        ],
        out_specs=out_specs,
        out_shape=out_shape,
        scratch_shapes=[pltpu.VMEM((tm, d), BF16)],
        compiler_params=_cparams(("parallel", "arbitrary")),
        name="proj",
    )(h, g, w, gain, segm)
    return res if aux else res[0]


def _outproj_kernel(o_ref, w_ref, h_ref, out_ref):
    out_ref[...] = h_ref[...] + jnp.dot(o_ref[...], w_ref[...], preferred_element_type=F32)


def _outproj(o, w, h):
    t, d = h.shape
    tm = min(1024, t)
    return pl.pallas_call(
        _outproj_kernel,
        grid=(t // tm,),
        in_specs=[
            pl.BlockSpec((tm, o.shape[1]), lambda i: (i, 0)),
            pl.BlockSpec(w.shape, lambda i: (0, 0)),
            pl.BlockSpec((tm, d), lambda i: (i, 0)),
        ],
        out_specs=pl.BlockSpec((tm, d), lambda i: (i, 0)),
        out_shape=jax.ShapeDtypeStruct((t, d), F32),
        compiler_params=_cparams(("parallel",)),
        name="outproj",
    )(o, w, h)


def _mlp_kernel(h_ref, g_ref, w1_ref, w2_ref, o_ref, u_ref, acc_ref):
    j = pl.program_id(1)

    @pl.when(j == 0)
    def _():
        u_ref[...] = _rms_rows(h_ref[...], g_ref[...]).astype(BF16)
        acc_ref[...] = jnp.zeros_like(acc_ref)

    hid = jnp.dot(u_ref[...], w1_ref[...], preferred_element_type=F32)
    hid = jnp.square(jnp.maximum(hid, 0.0)).astype(BF16)
    acc_ref[...] += jnp.dot(hid, w2_ref[...], preferred_element_type=F32)

    @pl.when(j == pl.num_programs(1) - 1)
    def _():
        o_ref[...] = h_ref[...] + acc_ref[...]


def _mlp(h, g, w1, w2):
    t, d = h.shape
    f = w1.shape[1]
    tm = min(1024, t)
    tf = 512
    return pl.pallas_call(
        _mlp_kernel,
        grid=(t // tm, f // tf),
        in_specs=[
            pl.BlockSpec((tm, d), lambda i, j: (i, 0)),
            pl.BlockSpec((1, d), lambda i, j: (0, 0)),
            pl.BlockSpec((d, tf), lambda i, j: (0, j)),
            pl.BlockSpec((tf, d), lambda i, j: (j, 0)),
        ],
        out_specs=pl.BlockSpec((tm, d), lambda i, j: (i, 0)),
        out_shape=jax.ShapeDtypeStruct((t, d), F32),
        scratch_shapes=[pltpu.VMEM((tm, d), BF16), pltpu.VMEM((tm, d), F32)],
        compiler_params=_cparams(("parallel", "arbitrary")),
        name="mlp",
    )(h, g, w1, w2)


def _softmax_step(s, vt, m_ref, l_ref, acc_ref, tk):
    m_prev = m_ref[...]
    m_next = jnp.maximum(m_prev, jnp.max(s, axis=1, keepdims=True))
    p = jnp.exp(s - pltpu.repeat(m_next, tk // LANES, axis=1))
    alpha = jnp.exp(m_prev - m_next)
    l_ref[...] = alpha * l_ref[...] + jnp.sum(p, axis=1, keepdims=True)
    m_ref[...] = m_next
    acc_ref[...] = alpha * acc_ref[...] + jnp.dot(p.astype(BF16), vt, preferred_element_type=F32)


def _attn_a_kernel(q_ref, k_ref, v_ref, lq1_ref, lk1_ref, lq2_ref, lk2_ref, subg_ref, o_ref,
                   m_ref, l_ref, acc_ref, *, tq, tk, lambda_init):
    i = pl.program_id(2)
    q = q_ref[...].astype(F32)
    lane = lax.broadcasted_iota(jnp.int32, (tq, LANES), 1)
    qm = jnp.concatenate([jnp.where(lane < A_HEAD_DIM, q, 0.0),
                          jnp.where(lane >= A_HEAD_DIM, q, 0.0)], axis=0).astype(BF16)
    m_ref[...] = jnp.full(m_ref.shape, NEG, F32)
    l_ref[...] = jnp.zeros_like(l_ref)
    acc_ref[...] = jnp.zeros_like(acc_ref)

    def step(j, masked):
        ks = pl.multiple_of(j * tk, tk)
        s = _dot_nt(qm, k_ref[pl.ds(ks, tk), :])
        if masked:
            row = lax.broadcasted_iota(jnp.int32, (2 * tq, tk), 0) & (tq - 1)
            col = lax.broadcasted_iota(jnp.int32, (2 * tq, tk), 1)
            ok = ((ks + col) >> CHUNK_SHIFT) <= ((i * tq + row) >> CHUNK_SHIFT)
            s = jnp.where(ok, s, NEG)
        _softmax_step(s, v_ref[pl.ds(ks, tk), :], m_ref, l_ref, acc_ref, tk)

    n_full = (i * tq) // tk
    n_tot = ((i + 1) * tq + tk - 1) // tk

    def full_body(j, c):
        step(j, False)
        return c

    def diag_body(j, c):
        step(j, True)
        return c

    lax.fori_loop(0, n_full, full_body, 0)
    lax.fori_loop(n_full, n_tot, diag_body, 0)

    o = acc_ref[...] / l_ref[...]
    lam = (jnp.exp(jnp.sum(lq1_ref[...] * lk1_ref[...], axis=1, keepdims=True))
           - jnp.exp(jnp.sum(lq2_ref[...] * lk2_ref[...], axis=1, keepdims=True)) + lambda_init)
    od = o[:tq] - lam * o[tq:]
    o_ref[...] = (_rms_rows(od, subg_ref[...]) * (1.0 - lambda_init)).astype(BF16)


def _attn_a(qkv, lq1, lk1, lq2, lk2, subg, *, batch, seq, lambda_init):
    t = batch * seq
    tq = min(256, seq)
    tk = min(512, seq)
    nq = seq // tq
    vec = lambda n: pl.BlockSpec((1, n), lambda b, h, i: (0, 0))
    return pl.pallas_call(
        functools.partial(_attn_a_kernel, tq=tq, tk=tk, lambda_init=lambda_init),
        grid=(batch, A_HEADS, nq),
        in_specs=[
            pl.BlockSpec((tq, LANES), lambda b, h, i: (b * nq + i, h)),
            pl.BlockSpec((seq, LANES), lambda b, h, i: (b, A_HEADS + h)),
            pl.BlockSpec((seq, LANES), lambda b, h, i: (b, 2 * A_HEADS + h)),
            vec(A_HEAD_DIM), vec(A_HEAD_DIM), vec(A_HEAD_DIM), vec(A_HEAD_DIM), vec(A_V_DIM),
        ],
        out_specs=pl.BlockSpec((tq, LANES), lambda b, h, i: (b * nq + i, h)),
        out_shape=jax.ShapeDtypeStruct((t, A_HEADS * A_V_DIM), BF16),
        scratch_shapes=[pltpu.VMEM((2 * tq, LANES), F32)] * 3,
        compiler_params=_cparams(("parallel", "parallel", "arbitrary")),
        name="attn_a",
    )(qkv, qkv, qkv, lq1, lk1, lq2, lk2, subg)


def _pool_kernel(h_ref, halo_ref, g_ref, wg_ref, sc_ref, o_ref, ext_ref, *, ts):
    i = pl.program_id(1)
    x = h_ref[...]
    u = _rms_rows(x, g_ref[...])
    uh = _rms_rows(halo_ref[...], g_ref[...])
    ext_ref[0:POOL_HALO, :] = jnp.where(i > 0, uh, 0.0)
    ext_ref[POOL_HALO:, :] = u
    pos = i * ts + lax.broadcasted_iota(jnp.int32, (ts, 1), 0)
    for g, w in enumerate(POOL_WINDOWS):
        cols = slice(g * POOL_GROUP, (g + 1) * POOL_GROUP)
        win = ext_ref[POOL_HALO:, cols]
        for d in range(1, w):
            win = win + ext_ref[pl.ds(POOL_HALO - d, ts), cols]
        count = jnp.minimum(pos + 1, w).astype(F32)
        pooled = win / count - u[:, cols]
        y = jnp.dot(pooled.astype(BF16), wg_ref[g], preferred_element_type=F32)
        o_ref[:, cols] = x[:, cols] + y * sc_ref[:, cols]


def _pool(h, g, wg, scale, *, batch, seq):
    t, d = h.shape
    ts = min(512, seq)
    ns = seq // ts
    hb = ts // POOL_HALO
    return pl.pallas_call(
        functools.partial(_pool_kernel, ts=ts),
        grid=(batch, ns),
        in_specs=[
            pl.BlockSpec((ts, d), lambda b, i: (b * ns + i, 0)),
            pl.BlockSpec((POOL_HALO, d), lambda b, i: (jnp.maximum((b * ns + i) * hb - 1, 0), 0)),
            pl.BlockSpec((1, d), lambda b, i: (0, 0)),
            pl.BlockSpec(wg.shape, lambda b, i: (0, 0, 0)),
            pl.BlockSpec((1, d), lambda b, i: (0, 0)),
        ],
        out_specs=pl.BlockSpec((ts, d), lambda b, i: (b * ns + i, 0)),
        out_shape=jax.ShapeDtypeStruct((t, d), F32),
        scratch_shapes=[pltpu.VMEM((ts + POOL_HALO, d), F32)],
        compiler_params=_cparams(("parallel", "arbitrary")),
        name="pool",
    )(h, h, g, wg, scale)


def _masked_heads(x, n_pairs, rows):
    xf = x.astype(F32)
    lane = lax.broadcasted_iota(jnp.int32, (rows, LANES), 1)
    out = []
    for p in range(n_pairs):
        xp = xf[:, p * LANES:(p + 1) * LANES]
        out.append(jnp.where(lane < IDX_DIM, xp, 0.0).astype(BF16))
        out.append(jnp.where(lane >= IDX_DIM, xp, 0.0).astype(BF16))
    return out


def _sum_sublane_groups(m, rows):
    parts = [m[r:r + 8] for r in range(0, rows, 8)]
    while len(parts) > 1:
        parts = [parts[a] + parts[a + 1] for a in range(0, len(parts), 2)]
    return parts[0]


def _index_thr_kernel(iq_ref, ik_ref, iw_ref, o_ref, sc_ref, *, tq, kb, topk, idx_scale):
    i = pl.program_id(1)
    nblk = ((i + 1) * tq) // kb
    iw = iw_ref[0] * idx_scale
    iqh = _masked_heads(iq_ref[...], IDX_HEADS // 2, tq)
    qchunk = (i * tq + lax.broadcasted_iota(jnp.int32, (kb, tq), 1)) >> CHUNK_SHIFT
    krow = lax.broadcasted_iota(jnp.int32, (kb, tq), 0)

    def score_body(r, c):
        ks = pl.multiple_of(r * kb, kb)
        ikb = ik_ref[pl.ds(ks, kb), :]
        acc = jnp.zeros((kb, tq), F32)
        for h in range(IDX_HEADS):
            acc = acc + iw[h:h + 1, :] * jnp.maximum(_dot_nt(ikb, iqh[h]), 0.0)
        acc = jnp.where(((ks + krow) >> CHUNK_SHIFT) <= qchunk, acc, -jnp.inf)
        sc_ref[pl.ds(ks, kb), :] = acc
        return c

    lax.fori_loop(0, nblk, score_body, 0)

    def count(pred):
        def body(r, a):
            ks = pl.multiple_of(r * kb, kb)
            m = jnp.where(pred(sc_ref[pl.ds(ks, kb), :], ks), 1.0, 0.0)
            return a + _sum_sublane_groups(m, kb)
        a = lax.fori_loop(0, nblk, body, jnp.zeros((8, tq), F32))
        return jnp.sum(a, axis=0, keepdims=True)

    int_min = jnp.int32(-2 ** 31)

    def key_to_f32(key_u):
        ks_ = key_u ^ int_min
        bits = jnp.where(ks_ >= 0, ks_, ks_ ^ jnp.int32(0x7FFFFFFF))
        return lax.bitcast_convert_type(bits, F32)

    def bit_body(b, t_u):
        cand_u = t_u | lax.shift_left(jnp.int32(1), 31 - b)
        cand = key_to_f32(cand_u)
        cnt = count(lambda blk, ks: blk >= cand)
        return jnp.where(cnt >= topk, cand_u, t_u)

    t_u = lax.fori_loop(0, 32, bit_body, jnp.zeros((1, tq), jnp.int32))
    thr = key_to_f32(t_u)
    thr = jnp.where(thr != thr, -jnp.inf, thr)
    need = topk - count(lambda blk, ks: blk > thr)

    def idx_body(b, j_u):
        cand = j_u | lax.shift_left(jnp.int32(1), 11 - b)
        cnt = count(lambda blk, ks: (blk == thr) & ((ks + krow) < cand))
        return jnp.where(cnt < need, cand, j_u)

    j_u = lax.fori_loop(0, 12, idx_body, jnp.zeros((1, tq), jnp.int32))
    rowid = lax.broadcasted_iota(jnp.int32, (8, tq), 0)
    o_ref[0] = jnp.where(rowid == 0, thr, j_u.astype(F32))


def _index_thr(proj, iw_t, *, batch, seq, topk):
    tq = min(256, seq)
    nq = seq // tq
    return pl.pallas_call(
        functools.partial(_index_thr_kernel, tq=tq, kb=tq, topk=float(topk),
                          idx_scale=IDX_HEADS ** -0.5 * IDX_DIM ** -0.5),
        grid=(batch, nq),
        in_specs=[
            pl.BlockSpec((tq, IDX_HEADS * IDX_DIM), lambda b, i: (b * nq + i, C_IQ0 // (IDX_HEADS * IDX_DIM))),
            pl.BlockSpec((seq, LANES), lambda b, i: (b, C_IK0 // LANES)),
            pl.BlockSpec((1, 8, tq), lambda b, i: (b, 0, i)),
        ],
        out_specs=pl.BlockSpec((1, 8, tq), lambda b, i: (b, 0, i)),
        out_shape=jax.ShapeDtypeStruct((batch, 8, seq), F32),
        scratch_shapes=[pltpu.VMEM((seq, tq), F32)],
        compiler_params=_cparams(("parallel", "arbitrary")),
        name="index_thr",
    )(proj, proj, iw_t)


def _attn_c_kernel(q_ref, k_ref, v_ref, iq_ref, ik_ref, aux_ref, thr_ref, o_ref,
                   m_ref, l_ref, acc_ref, *, tq, tk, idx_scale):
    i = pl.program_id(1)
    iqh = _masked_heads(iq_ref[...], IDX_HEADS // 2, tq)
    iw = aux_ref[:, 0:IDX_HEADS] * idx_scale
    thr = thr_ref[0][:, 0:1]
    jcut = thr_ref[0][:, 1:2].astype(jnp.int32)
    m_ref[...] = jnp.full(m_ref.shape, NEG, F32)
    l_ref[...] = jnp.zeros_like(l_ref)
    acc_ref[...] = jnp.zeros_like(acc_ref)

    def step(j, masked):
        ks = pl.multiple_of(j * tk, tk)
        ikb = ik_ref[pl.ds(ks, tk), :]
        score = jnp.zeros((tq, tk), F32)
        for h in range(IDX_HEADS):
            score = score + iw[:, h:h + 1] * jnp.maximum(_dot_nt(iqh[h], ikb), 0.0)
        kidx = ks + lax.broadcasted_iota(jnp.int32, (tq, tk), 1)
        sel = (score > thr) | ((score == thr) & (kidx <= jcut))
        if masked:
            qpos = i * tq + lax.broadcasted_iota(jnp.int32, (tq, tk), 0)
            sel = sel & ((kidx >> CHUNK_SHIFT) <= (qpos >> CHUNK_SHIFT))
        for g in range(C_KV_HEADS):
            kt = k_ref[pl.ds(ks, tk), g * C_HEAD_DIM:(g + 1) * C_HEAD_DIM]
            vt = v_ref[pl.ds(ks, tk), g * C_HEAD_DIM:(g + 1) * C_HEAD_DIM]
            for r in range(C_HEADS // C_KV_HEADS):
                hh = g * (C_HEADS // C_KV_HEADS) + r
                s = _dot_nt(q_ref[:, hh * C_HEAD_DIM:(hh + 1) * C_HEAD_DIM], kt)
                s = jnp.where(sel, s, NEG)
                _softmax_step(s, vt, m_ref.at[hh], l_ref.at[hh], acc_ref.at[hh], tk)

    n_full = (i * tq) // tk
    n_tot = ((i + 1) * tq + tk - 1) // tk

    def full_body(j, c):
        step(j, False)
        return c

    def diag_body(j, c):
        step(j, True)
        return c

    lax.fori_loop(0, n_full, full_body, 0)
    lax.fori_loop(n_full, n_tot, diag_body, 0)
    for hh in range(C_HEADS):
        o_ref[:, hh * C_HEAD_DIM:(hh + 1) * C_HEAD_DIM] = (acc_ref[hh] / l_ref[hh]).astype(BF16)


def _attn_c(proj, aux, thr, *, batch, seq):
    t = batch * seq
    tq = min(256, seq)
    tk = min(256, seq)
    nq = seq // tq
    kvw = C_KV_HEADS * C_HEAD_DIM
    return pl.pallas_call(
        functools.partial(_attn_c_kernel, tq=tq, tk=tk, idx_scale=IDX_HEADS ** -0.5 * IDX_DIM ** -0.5),
        grid=(batch, nq),
        in_specs=[
            pl.BlockSpec((tq, C_HEADS * C_HEAD_DIM), lambda b, i: (b * nq + i, 0)),
            pl.BlockSpec((seq, kvw), lambda b, i: (b, C_K0 // kvw)),
            pl.BlockSpec((seq, kvw), lambda b, i: (b, C_V0 // kvw)),
            pl.BlockSpec((tq, IDX_HEADS * IDX_DIM), lambda b, i: (b * nq + i, C_IQ0 // (IDX_HEADS * IDX_DIM))),
            pl.BlockSpec((seq, LANES), lambda b, i: (b, C_IK0 // LANES)),
            pl.BlockSpec((tq, LANES), lambda b, i: (b * nq + i, 0)),
            pl.BlockSpec((1, tq, 8), lambda b, i: (b, i, 0)),
        ],
        out_specs=pl.BlockSpec((tq, C_HEADS * C_HEAD_DIM), lambda b, i: (b * nq + i, 0)),
        out_shape=jax.ShapeDtypeStruct((t, C_HEADS * C_HEAD_DIM), BF16),
        scratch_shapes=[pltpu.VMEM((C_HEADS, tq, LANES), F32)] * 3,
        compiler_params=_cparams(("parallel", "arbitrary")),
        name="attn_c",
    )(proj, proj, proj, proj, proj, aux, thr)


def _pad_c_weight(w):
    iw = w[:, 2112:2120]
    ik = w[:, 2048:2112]
    pad = jnp.zeros((w.shape[0], C_COLS - C_IW0 - IDX_HEADS), w.dtype)
    return jnp.concatenate([w[:, :2048], ik, ik, iw, pad], axis=1).astype(BF16)


def kernel(x, norm1_g, norm2_g, a_w_in, a_q_norm_g, a_k_norm_g, a_lambda_q1, a_lambda_k1, a_lambda_q2,
           a_lambda_k2, a_subln_g, a_w_out, b_w_group, b_scale, c_w_in, c_q_norm_g, c_k_norm_g, c_w_out,
           mlp_w1, mlp_w2):
    batch, seq, d = x.shape
    depth = norm1_g.shape[0]
    h = x.reshape(batch * seq, d)
    row = lambda v: v.reshape(1, -1).astype(F32)
    for i in range(depth):
        m, j = i % N_MIXERS, i // N_MIXERS
        g1 = row(norm1_g[i])
        if m == 0:
            lambda_init = 0.8 - 0.6 * math.exp(-0.3 * i)
            gain = jnp.concatenate([jnp.tile(a_q_norm_g[j], 2 * A_HEADS) * A_HEAD_DIM ** -0.5,
                                    jnp.tile(a_k_norm_g[j], 2 * A_HEADS),
                                    jnp.ones((A_HEADS * A_V_DIM,), F32)]).reshape(1, -1)
            qkv = _proj(h, g1, a_w_in[j].astype(BF16), gain, n_norm_cols=2 * D_MODEL, seg=A_HEAD_DIM, aux=False)
            o = _attn_a(qkv, row(a_lambda_q1[j]), row(a_lambda_k1[j]), row(a_lambda_q2[j]), row(a_lambda_k2[j]),
                        row(a_subln_g[j]), batch=batch, seq=seq, lambda_init=lambda_init)
            h = _outproj(o, a_w_out[j].astype(BF16), h)
        elif m == 1:
            h = _pool(h, g1, b_w_group[j].astype(BF16), row(b_scale[j]), batch=batch, seq=seq)
        else:
            gain = jnp.concatenate([jnp.tile(c_q_norm_g[j], C_HEADS) * C_HEAD_DIM ** -0.5,
                                    jnp.tile(c_k_norm_g[j], C_KV_HEADS),
                                    jnp.ones((C_COLS - C_V0,), F32)]).reshape(1, -1)
            proj, aux = _proj(h, g1, _pad_c_weight(c_w_in[j]), gain, n_norm_cols=C_V0, seg=C_HEAD_DIM, aux=True)
            iw_t = aux[:, :IDX_HEADS].reshape(batch, seq, IDX_HEADS).transpose(0, 2, 1)
            thr = _index_thr(proj, iw_t, batch=batch, seq=seq, topk=min(TOPK_MAX, seq // 4))
            o = _attn_c(proj, aux, thr.transpose(0, 2, 1), batch=batch, seq=seq)
            h = _outproj(o, c_w_out[j].astype(BF16), h)
        h = _mlp(h, row(norm2_g[i]), mlp_w1[i].astype(BF16), mlp_w2[i].astype(BF16))
    return h.reshape(batch, seq, d)
```

```python
import functools
import math

import jax
import jax.numpy as jnp
from jax import lax
from jax.experimental import pallas as pl
from jax.experimental.pallas import tpu as pltpu

F32 = jnp.float32
BF16 = jnp.bfloat16

D_MODEL = 1024
N_MIXERS = 3
CHUNK = 64
CHUNK_SHIFT = 6
EPS = 1e-6
LANES = 128
MXU_DIM = 256

A_HEADS = 8
A_HEAD_DIM = 64
A_V_DIM = 128
POOL_WINDOWS = (2, 4, 8, 16)
POOL_GROUP = 256
POOL_HALO = 16
C_HEADS = 8
C_HEAD_DIM = 128
C_KV_HEADS = 2
IDX_HEADS = 8
IDX_DIM = 64
TOPK_MAX = 256

C_Q0, C_K0, C_V0, C_IQ0, C_IK0, C_IW0, C_COLS = 0, 1024, 1280, 1536, 2048, 2176, 2304

NEG = -1e30
LOG2E = math.log2(math.e)
VMEM_LIMIT = 56 * 1024 * 1024


def _cparams(sem):
    return pltpu.CompilerParams(dimension_semantics=sem, vmem_limit_bytes=VMEM_LIMIT)


def _rms_rows(x, g):
    ms = jnp.mean(x * x, axis=-1, keepdims=True)
    return (x * lax.rsqrt(ms + EPS)) * g


def _dot_nt(a, b):
    return lax.dot_general(a, b, (((1,), (1,)), ((), ())), preferred_element_type=F32)


def _proj_kernel(h_ref, g_ref, w_ref, gain_ref, seg_ref, o_ref, *rest, n_norm_tiles, seg, tn, aux):
    if aux:
        aux_ref, u_ref = rest
    else:
        (u_ref,) = rest
    j = pl.program_id(1)

    @pl.when(j == 0)
    def _():
        u_ref[...] = _rms_rows(h_ref[...], g_ref[...]).astype(BF16)

    acc = jnp.dot(u_ref[...], w_ref[...], preferred_element_type=F32)

    @pl.when(j < n_norm_tiles)
    def _():
        for c in range(tn // MXU_DIM):
            a = acc[:, c * MXU_DIM:(c + 1) * MXU_DIM]
            sq = a * a
            hi = sq.astype(BF16)
            lo = (sq - hi.astype(F32)).astype(BF16)
            ssum = (jnp.dot(hi, seg_ref[...], preferred_element_type=F32)
                    + jnp.dot(lo, seg_ref[...], preferred_element_type=F32))
            y = (a * lax.rsqrt(ssum * (1.0 / seg) + EPS)) * gain_ref[:, c * MXU_DIM:(c + 1) * MXU_DIM]
            o_ref[:, c * MXU_DIM:(c + 1) * MXU_DIM] = y.astype(BF16)

    @pl.when(j >= n_norm_tiles)
    def _():
        o_ref[...] = acc.astype(BF16)

    if aux:
        @pl.when(j == pl.num_programs(1) - 1)
        def _():
            aux_ref[...] = acc[:, tn - LANES:]


def _proj(h, g, w, gain, *, n_norm_cols, seg, aux):
    t, d = h.shape
    n = w.shape[1]
    tm = min(1024, t)
    tn = 512 if n % 512 == 0 else 256
    assert n % tn == 0 and n_norm_cols % tn == 0 and t % tm == 0
    r = lax.broadcasted_iota(jnp.int32, (MXU_DIM, MXU_DIM), 0) // seg
    c = lax.broadcasted_iota(jnp.int32, (MXU_DIM, MXU_DIM), 1) // seg
    segm = (r == c).astype(BF16)
    out_shape = [jax.ShapeDtypeStruct((t, n), BF16)]
    out_specs = [pl.BlockSpec((tm, tn), lambda i, j: (i, j))]
    if aux:
        out_shape.append(jax.ShapeDtypeStruct((t, LANES), F32))
        out_specs.append(pl.BlockSpec((tm, LANES), lambda i, j: (i, 0)))
    res = pl.pallas_call(
        functools.partial(_proj_kernel, n_norm_tiles=n_norm_cols // tn, seg=seg, tn=tn, aux=aux),
        grid=(t // tm, n // tn),
        in_specs=[
            pl.BlockSpec((tm, d), lambda i, j: (i, 0)),
            pl.BlockSpec((1, d), lambda i, j: (0, 0)),
            pl.BlockSpec((d, tn), lambda i, j: (0, j)),
            pl.BlockSpec((1, tn), lambda i, j: (0, j)),
            pl.BlockSpec((MXU_DIM, MXU_DIM), lambda i, j: (0, 0)),
        ],
        out_specs=out_specs,
        out_shape=out_shape,
        scratch_shapes=[pltpu.VMEM((tm, d), BF16)],
        compiler_params=_cparams(("parallel", "arbitrary")),
        name="proj",
    )(h, g, w, gain, segm)
    return res if aux else res[0]


def _outproj_kernel(o_ref, w_ref, h_ref, out_ref):
    out_ref[...] = h_ref[...] + jnp.dot(o_ref[...], w_ref[...], preferred_element_type=F32)


def _outproj(o, w, h):
    t, d = h.shape
    tm = min(1024, t)
    return pl.pallas_call(
        _outproj_kernel,
        grid=(t // tm,),
        in_specs=[
            pl.BlockSpec((tm, o.shape[1]), lambda i: (i, 0)),
            pl.BlockSpec(w.shape, lambda i: (0, 0)),
            pl.BlockSpec((tm, d), lambda i: (i, 0)),
        ],
        out_specs=pl.BlockSpec((tm, d), lambda i: (i, 0)),
        out_shape=jax.ShapeDtypeStruct((t, d), F32),
        compiler_params=_cparams(("parallel",)),
        name="outproj",
    )(o, w, h)


def _mlp_kernel(h_ref, g_ref, w1_ref, w2_ref, o_ref, u_ref, acc_ref):
    j = pl.program_id(1)

    @pl.when(j == 0)
    def _():
        u_ref[...] = _rms_rows(h_ref[...], g_ref[...]).astype(BF16)
        acc_ref[...] = jnp.zeros_like(acc_ref)

    hid = jnp.dot(u_ref[...], w1_ref[...], preferred_element_type=F32)
    hid = jnp.square(jnp.maximum(hid, 0.0)).astype(BF16)
    acc_ref[...] += jnp.dot(hid, w2_ref[...], preferred_element_type=F32)

    @pl.when(j == pl.num_programs(1) - 1)
    def _():
        o_ref[...] = h_ref[...] + acc_ref[...]


def _mlp(h, g, w1, w2):
    t, d = h.shape
    f = w1.shape[1]
    tm = min(1024, t)
    tf = 512
    return pl.pallas_call(
        _mlp_kernel,
        grid=(t // tm, f // tf),
        in_specs=[
            pl.BlockSpec((tm, d), lambda i, j: (i, 0)),
            pl.BlockSpec((1, d), lambda i, j: (0, 0)),
            pl.BlockSpec((d, tf), lambda i, j: (0, j)),
            pl.BlockSpec((tf, d), lambda i, j: (j, 0)),
        ],
        out_specs=pl.BlockSpec((tm, d), lambda i, j: (i, 0)),
        out_shape=jax.ShapeDtypeStruct((t, d), F32),
        scratch_shapes=[pltpu.VMEM((tm, d), BF16), pltpu.VMEM((tm, d), F32)],
        compiler_params=_cparams(("parallel", "arbitrary")),
        name="mlp",
    )(h, g, w1, w2)


def _softmax_step(s, v1, m_ref, l_ref, acc_ref, tk):
    m_prev = m_ref[...]
    m_next = jnp.maximum(m_prev, jnp.max(s, axis=1, keepdims=True))
    p = jnp.exp2(s - pltpu.repeat(m_next, tk // LANES, axis=1))
    alpha = jnp.exp2(m_prev - m_next)
    pv = jnp.dot(p.astype(BF16), v1, preferred_element_type=F32)
    l_ref[...] = alpha * l_ref[...] + pv[:, LANES:]
    m_ref[...] = m_next
    acc_ref[...] = alpha * acc_ref[...] + pv[:, :LANES]


def _attn_a_kernel(q_ref, k_ref, v_ref, lq1_ref, lk1_ref, lq2_ref, lk2_ref, subg_ref, o_ref,
                   m_ref, l_ref, acc_ref, s0_ref, s1_ref, *, tq, tk, lambda_init):
    i = pl.program_id(2)
    q = q_ref[...].astype(F32)
    lane = lax.broadcasted_iota(jnp.int32, (tq, LANES), 1)
    qm = jnp.concatenate([jnp.where(lane < A_HEAD_DIM, q, 0.0),
                          jnp.where(lane >= A_HEAD_DIM, q, 0.0)], axis=0).astype(BF16)
    m_ref[...] = jnp.full(m_ref.shape, NEG, F32)
    l_ref[...] = jnp.zeros_like(l_ref)
    acc_ref[...] = jnp.zeros_like(acc_ref)
    ones = jnp.ones((tk, LANES), BF16)

    def scores(j, buf):
        ks = pl.multiple_of(j * tk, tk)
        buf[...] = _dot_nt(qm, k_ref[pl.ds(ks, tk), :])

    def consume(j, buf, masked):
        ks = pl.multiple_of(j * tk, tk)
        s = buf[...]
        if masked:
            row = lax.broadcasted_iota(jnp.int32, (2 * tq, tk), 0) & (tq - 1)
            col = lax.broadcasted_iota(jnp.int32, (2 * tq, tk), 1)
            ok = ((ks + col) >> CHUNK_SHIFT) <= ((i * tq + row) >> CHUNK_SHIFT)
            s = jnp.where(ok, s, NEG)
        v1 = jnp.concatenate([v_ref[pl.ds(ks, tk), :], ones], axis=1)
        _softmax_step(s, v1, m_ref, l_ref, acc_ref, tk)

    n_tot = ((i + 1) * tq + tk - 1) // tk
    n_pairs = (n_tot - 1) // 2
    scores(0, s0_ref)

    def pair(t, c):
        scores(2 * t + 1, s1_ref)
        consume(2 * t, s0_ref, False)
        scores(2 * t + 2, s0_ref)
        consume(2 * t + 1, s1_ref, False)
        return c

    lax.fori_loop(0, n_pairs, pair, 0)

    @pl.when(n_tot - 2 * n_pairs == 1)
    def _():
        consume(n_tot - 1, s0_ref, True)

    @pl.when(n_tot - 2 * n_pairs == 2)
    def _():
        scores(n_tot - 1, s1_ref)
        consume(n_tot - 2, s0_ref, False)
        consume(n_tot - 1, s1_ref, True)

    o = acc_ref[...] / l_ref[...]
    lam = (jnp.exp(jnp.sum(lq1_ref[...] * lk1_ref[...], axis=1, keepdims=True))
           - jnp.exp(jnp.sum(lq2_ref[...] * lk2_ref[...], axis=1, keepdims=True)) + lambda_init)
    od = o[:tq] - lam * o[tq:]
    o_ref[...] = (_rms_rows(od, subg_ref[...]) * (1.0 - lambda_init)).astype(BF16)


def _attn_a(qkv, lq1, lk1, lq2, lk2, subg, *, batch, seq, lambda_init):
    t = batch * seq
    tq = min(256, seq)
    tk = min(512, seq)
    nq = seq // tq
    vec = lambda n: pl.BlockSpec((1, n), lambda b, h, i: (0, 0))
    return pl.pallas_call(
        functools.partial(_attn_a_kernel, tq=tq, tk=tk, lambda_init=lambda_init),
        grid=(batch, A_HEADS, nq),
        in_specs=[
            pl.BlockSpec((tq, LANES), lambda b, h, i: (b * nq + i, h)),
            pl.BlockSpec((seq, LANES), lambda b, h, i: (b, A_HEADS + h)),
            pl.BlockSpec((seq, LANES), lambda b, h, i: (b, 2 * A_HEADS + h)),
            vec(A_HEAD_DIM), vec(A_HEAD_DIM), vec(A_HEAD_DIM), vec(A_HEAD_DIM), vec(A_V_DIM),
        ],
        out_specs=pl.BlockSpec((tq, LANES), lambda b, h, i: (b * nq + i, h)),
        out_shape=jax.ShapeDtypeStruct((t, A_HEADS * A_V_DIM), BF16),
        scratch_shapes=[pltpu.VMEM((2 * tq, LANES), F32)] * 3 + [pltpu.VMEM((2 * tq, tk), F32)] * 2,
        compiler_params=_cparams(("parallel", "parallel", "arbitrary")),
        name="attn_a",
    )(qkv, qkv, qkv, lq1, lk1, lq2, lk2, subg)


def _pool_kernel(h_ref, halo_ref, g_ref, wg_ref, sc_ref, o_ref, ext_ref, *, ts):
    i = pl.program_id(1)
    x = h_ref[...]
    u = _rms_rows(x, g_ref[...])
    uh = _rms_rows(halo_ref[...], g_ref[...])
    ext_ref[0:POOL_HALO, :] = jnp.where(i > 0, uh, 0.0)
    ext_ref[POOL_HALO:, :] = u
    pos = i * ts + lax.broadcasted_iota(jnp.int32, (ts, 1), 0)
    for g, w in enumerate(POOL_WINDOWS):
        cols = slice(g * POOL_GROUP, (g + 1) * POOL_GROUP)
        win = ext_ref[POOL_HALO:, cols]
        for d in range(1, w):
            win = win + ext_ref[pl.ds(POOL_HALO - d, ts), cols]
        count = jnp.minimum(pos + 1, w).astype(F32)
        pooled = win / count - u[:, cols]
        y = jnp.dot(pooled.astype(BF16), wg_ref[g], preferred_element_type=F32)
        o_ref[:, cols] = x[:, cols] + y * sc_ref[:, cols]


def _pool(h, g, wg, scale, *, batch, seq):
    t, d = h.shape
    ts = min(512, seq)
    ns = seq // ts
    hb = ts // POOL_HALO
    return pl.pallas_call(
        functools.partial(_pool_kernel, ts=ts),
        grid=(batch, ns),
        in_specs=[
            pl.BlockSpec((ts, d), lambda b, i: (b * ns + i, 0)),
            pl.BlockSpec((POOL_HALO, d), lambda b, i: (jnp.maximum((b * ns + i) * hb - 1, 0), 0)),
            pl.BlockSpec((1, d), lambda b, i: (0, 0)),
            pl.BlockSpec(wg.shape, lambda b, i: (0, 0, 0)),
            pl.BlockSpec((1, d), lambda b, i: (0, 0)),
        ],
        out_specs=pl.BlockSpec((ts, d), lambda b, i: (b * ns + i, 0)),
        out_shape=jax.ShapeDtypeStruct((t, d), F32),
        scratch_shapes=[pltpu.VMEM((ts + POOL_HALO, d), F32)],
        compiler_params=_cparams(("parallel", "arbitrary")),
        name="pool",
    )(h, h, g, wg, scale)


def _masked_heads(x, n_pairs, rows):
    xf = x.astype(F32)
    lane = lax.broadcasted_iota(jnp.int32, (rows, LANES), 1)
    out = []
    for p in range(n_pairs):
        xp = xf[:, p * LANES:(p + 1) * LANES]
        out.append(jnp.where(lane < IDX_DIM, xp, 0.0).astype(BF16))
        out.append(jnp.where(lane >= IDX_DIM, xp, 0.0).astype(BF16))
    return out


def _sum_sublane_groups(m, rows):
    parts = [m[r:r + 8] for r in range(0, rows, 8)]
    while len(parts) > 1:
        parts = [parts[a] + parts[a + 1] for a in range(0, len(parts), 2)]
    return parts[0]


def _index_thr_kernel(iq_ref, ik_ref, iw_ref, o_ref, sc_ref, *, tq, kb, topk, idx_bits, idx_scale):
    i = pl.program_id(1)
    nblk = ((i + 1) * tq) // kb
    iw = iw_ref[0] * idx_scale
    iqh = _masked_heads(iq_ref[...], IDX_HEADS // 2, tq)
    qchunk = (i * tq + lax.broadcasted_iota(jnp.int32, (kb, tq), 1)) >> CHUNK_SHIFT
    krow = lax.broadcasted_iota(jnp.int32, (kb, tq), 0)

    def score_body(r, c):
        ks = pl.multiple_of(r * kb, kb)
        ikb = ik_ref[pl.ds(ks, kb), :]
        acc = jnp.zeros((kb, tq), F32)
        for h in range(IDX_HEADS):
            acc = acc + iw[h:h + 1, :] * jnp.maximum(_dot_nt(ikb, iqh[h]), 0.0)
        acc = jnp.where(((ks + krow) >> CHUNK_SHIFT) <= qchunk, acc, -jnp.inf)
        sc_ref[pl.ds(ks, kb), :] = acc
        return c

    lax.fori_loop(0, nblk, score_body, 0)

    def count(pred):
        def body(r, a):
            ks = pl.multiple_of(r * kb, kb)
            m = jnp.where(pred(sc_ref[pl.ds(ks, kb), :], ks), 1.0, 0.0)
            return a + _sum_sublane_groups(m, kb)
        a = lax.fori_loop(0, nblk, body, jnp.zeros((8, tq), F32))
        return jnp.sum(a, axis=0, keepdims=True)

    int_min = jnp.int32(-2 ** 31)

    def key_to_f32(key_u):
        ks_ = key_u ^ int_min
        bits = jnp.where(ks_ >= 0, ks_, ks_ ^ jnp.int32(0x7FFFFFFF))
        return lax.bitcast_convert_type(bits, F32)

    def bit_body(b, t_u):
        cand_u = t_u | lax.shift_left(jnp.int32(1), 31 - b)
        cand = key_to_f32(cand_u)
        cnt = count(lambda blk, ks: blk >= cand)
        return jnp.where(cnt >= topk, cand_u, t_u)

    t_u = lax.fori_loop(0, 32, bit_body, jnp.zeros((1, tq), jnp.int32))
    thr = key_to_f32(t_u)
    thr = jnp.where(thr != thr, -jnp.inf, thr)
    n_ge = count(lambda blk, ks: blk >= thr)

    def tie_cut():
        need = topk - count(lambda blk, ks: blk > thr)

        def idx_body(b, j_u):
            cand = j_u | lax.shift_left(jnp.int32(1), idx_bits - 1 - b)
            cnt = count(lambda blk, ks: (blk == thr) & ((ks + krow) < cand))
            return jnp.where(cnt < need, cand, j_u)

        return lax.fori_loop(0, idx_bits, idx_body, jnp.zeros((1, tq), jnp.int32))

    def no_cut():
        return jnp.full((1, tq), 2 ** idx_bits - 1, jnp.int32)

    j_u = lax.cond(jnp.max(n_ge) > topk, tie_cut, no_cut)
    rowid = lax.broadcasted_iota(jnp.int32, (8, tq), 0)
    o_ref[0] = jnp.where(rowid == 0, thr, j_u.astype(F32))


def _index_thr(proj, iw_t, *, batch, seq, topk):
    tq = min(256, seq)
    nq = seq // tq
    return pl.pallas_call(
        functools.partial(_index_thr_kernel, tq=tq, kb=tq, topk=float(topk), idx_bits=(seq - 1).bit_length(),
                          idx_scale=IDX_HEADS ** -0.5 * IDX_DIM ** -0.5),
        grid=(batch, nq),
        in_specs=[
            pl.BlockSpec((tq, IDX_HEADS * IDX_DIM), lambda b, i: (b * nq + i, C_IQ0 // (IDX_HEADS * IDX_DIM))),
            pl.BlockSpec((seq, LANES), lambda b, i: (b, C_IK0 // LANES)),
            pl.BlockSpec((1, 8, tq), lambda b, i: (b, 0, i)),
        ],
        out_specs=pl.BlockSpec((1, 8, tq), lambda b, i: (b, 0, i)),
        out_shape=jax.ShapeDtypeStruct((batch, 8, seq), F32),
        scratch_shapes=[pltpu.VMEM((seq, tq), F32)],
        compiler_params=_cparams(("parallel", "arbitrary")),
        name="index_thr",
    )(proj, proj, iw_t)


def _attn_c_kernel(q_ref, k_ref, v_ref, iq_ref, ik_ref, aux_ref, thr_ref, o_ref,
                   m_ref, l_ref, acc_ref, *, tq, tk, idx_scale):
    i = pl.program_id(1)
    iqh = _masked_heads(iq_ref[...], IDX_HEADS // 2, tq)
    wide = lambda col: pltpu.repeat(jnp.broadcast_to(col, (tq, LANES)), tk // LANES, axis=1)
    iw = aux_ref[:, 0:IDX_HEADS] * idx_scale
    iw_w = [wide(iw[:, h:h + 1]) for h in range(IDX_HEADS)]
    thr = wide(thr_ref[0][:, 0:1])
    jcut = wide(thr_ref[0][:, 1:2]).astype(jnp.int32)
    m_ref[...] = jnp.full(m_ref.shape, NEG, F32)
    l_ref[...] = jnp.zeros_like(l_ref)
    acc_ref[...] = jnp.zeros_like(acc_ref)
    ones = jnp.ones((tk, LANES), BF16)

    def step(j, masked):
        ks = pl.multiple_of(j * tk, tk)
        ikb = ik_ref[pl.ds(ks, tk), :]
        score = jnp.zeros((tq, tk), F32)
        for h in range(IDX_HEADS):
            score = score + iw_w[h] * jnp.maximum(_dot_nt(iqh[h], ikb), 0.0)
        kidx = ks + lax.broadcasted_iota(jnp.int32, (tq, tk), 1)
        sel = (score > thr) | ((score == thr) & (kidx <= jcut))
        if masked:
            qpos = i * tq + lax.broadcasted_iota(jnp.int32, (tq, tk), 0)
            sel = sel & ((kidx >> CHUNK_SHIFT) <= (qpos >> CHUNK_SHIFT))
        for g in range(C_KV_HEADS):
            kt = k_ref[pl.ds(ks, tk), g * C_HEAD_DIM:(g + 1) * C_HEAD_DIM]
            v1 = jnp.concatenate([v_ref[pl.ds(ks, tk), g * C_HEAD_DIM:(g + 1) * C_HEAD_DIM], ones], axis=1)
            for r in range(C_HEADS // C_KV_HEADS):
                hh = g * (C_HEADS // C_KV_HEADS) + r
                s = _dot_nt(q_ref[:, hh * C_HEAD_DIM:(hh + 1) * C_HEAD_DIM], kt)
                s = jnp.where(sel, s, NEG)
                _softmax_step(s, v1, m_ref.at[hh], l_ref.at[hh], acc_ref.at[hh], tk)

    n_full = (i * tq) // tk
    n_tot = ((i + 1) * tq + tk - 1) // tk

    def full_body(j, c):
        step(j, False)
        return c

    def diag_body(j, c):
        step(j, True)
        return c

    lax.fori_loop(0, n_full, full_body, 0)
    lax.fori_loop(n_full, n_tot, diag_body, 0)
    for hh in range(C_HEADS):
        o_ref[:, hh * C_HEAD_DIM:(hh + 1) * C_HEAD_DIM] = (acc_ref[hh] / l_ref[hh]).astype(BF16)


def _attn_c(proj, aux, thr, *, batch, seq):
    t = batch * seq
    tq = min(256, seq)
    tk = min(256, seq)
    nq = seq // tq
    kvw = C_KV_HEADS * C_HEAD_DIM
    return pl.pallas_call(
        functools.partial(_attn_c_kernel, tq=tq, tk=tk, idx_scale=IDX_HEADS ** -0.5 * IDX_DIM ** -0.5),
        grid=(batch, nq),
        in_specs=[
            pl.BlockSpec((tq, C_HEADS * C_HEAD_DIM), lambda b, i: (b * nq + i, 0)),
            pl.BlockSpec((seq, kvw), lambda b, i: (b, C_K0 // kvw)),
            pl.BlockSpec((seq, kvw), lambda b, i: (b, C_V0 // kvw)),
            pl.BlockSpec((tq, IDX_HEADS * IDX_DIM), lambda b, i: (b * nq + i, C_IQ0 // (IDX_HEADS * IDX_DIM))),
            pl.BlockSpec((seq, LANES), lambda b, i: (b, C_IK0 // LANES)),
            pl.BlockSpec((tq, LANES), lambda b, i: (b * nq + i, 0)),
            pl.BlockSpec((1, tq, 8), lambda b, i: (b, i, 0)),
        ],
        out_specs=pl.BlockSpec((tq, C_HEADS * C_HEAD_DIM), lambda b, i: (b * nq + i, 0)),
        out_shape=jax.ShapeDtypeStruct((t, C_HEADS * C_HEAD_DIM), BF16),
        scratch_shapes=[pltpu.VMEM((C_HEADS, tq, LANES), F32)] * 3,
        compiler_params=_cparams(("parallel", "arbitrary")),
        name="attn_c",
    )(proj, proj, proj, proj, proj, aux, thr)


def _pad_c_weight(w):
    iw = w[:, 2112:2120]
    ik = w[:, 2048:2112]
    pad = jnp.zeros((w.shape[0], C_COLS - C_IW0 - IDX_HEADS), w.dtype)
    return jnp.concatenate([w[:, :2048], ik, ik, iw, pad], axis=1).astype(BF16)


def kernel(x, norm1_g, norm2_g, a_w_in, a_q_norm_g, a_k_norm_g, a_lambda_q1, a_lambda_k1, a_lambda_q2,
           a_lambda_k2, a_subln_g, a_w_out, b_w_group, b_scale, c_w_in, c_q_norm_g, c_k_norm_g, c_w_out,
           mlp_w1, mlp_w2):
    batch, seq, d = x.shape
    depth = norm1_g.shape[0]
    h = x.reshape(batch * seq, d)
    row = lambda v: v.reshape(1, -1).astype(F32)
    for i in range(depth):
        m, j = i % N_MIXERS, i // N_MIXERS
        g1 = row(norm1_g[i])
        if m == 0:
            lambda_init = 0.8 - 0.6 * math.exp(-0.3 * i)
            gain = jnp.concatenate([jnp.tile(a_q_norm_g[j], 2 * A_HEADS) * (A_HEAD_DIM ** -0.5 * LOG2E),
                                    jnp.tile(a_k_norm_g[j], 2 * A_HEADS),
                                    jnp.ones((A_HEADS * A_V_DIM,), F32)]).reshape(1, -1)
            qkv = _proj(h, g1, a_w_in[j].astype(BF16), gain, n_norm_cols=2 * D_MODEL, seg=A_HEAD_DIM, aux=False)
            o = _attn_a(qkv, row(a_lambda_q1[j]), row(a_lambda_k1[j]), row(a_lambda_q2[j]), row(a_lambda_k2[j]),
                        row(a_subln_g[j]), batch=batch, seq=seq, lambda_init=lambda_init)
            h = _outproj(o, a_w_out[j].astype(BF16), h)
        elif m == 1:
            h = _pool(h, g1, b_w_group[j].astype(BF16), row(b_scale[j]), batch=batch, seq=seq)
        else:
            gain = jnp.concatenate([jnp.tile(c_q_norm_g[j], C_HEADS) * (C_HEAD_DIM ** -0.5 * LOG2E),
                                    jnp.tile(c_k_norm_g[j], C_KV_HEADS),
                                    jnp.ones((C_COLS - C_V0,), F32)]).reshape(1, -1)
            proj, aux = _proj(h, g1, _pad_c_weight(c_w_in[j]), gain, n_norm_cols=C_V0, seg=C_HEAD_DIM, aux=True)
            iw_t = aux[:, :IDX_HEADS].reshape(batch, seq, IDX_HEADS).transpose(0, 2, 1)
            thr = _index_thr(proj, iw_t, batch=batch, seq=seq, topk=min(TOPK_MAX, seq // 4))
            o = _attn_c(proj, aux, thr.transpose(0, 2, 1), batch=batch, seq=seq)
            h = _outproj(o, c_w_out[j].astype(BF16), h)
        h = _mlp(h, row(norm2_g[i]), mlp_w1[i].astype(BF16), mlp_w2[i].astype(BF16))
    return h.reshape(batch, seq, d)
```

```python
import functools
import math

import jax
import jax.numpy as jnp
from jax import lax
from jax.experimental import pallas as pl
from jax.experimental.pallas import tpu as pltpu

F32 = jnp.float32
BF16 = jnp.bfloat16

D_MODEL = 1024
N_MIXERS = 3
CHUNK = 64
CHUNK_SHIFT = 6
EPS = 1e-6
LANES = 128
MXU_DIM = 256

A_HEADS = 8
A_HEAD_DIM = 64
A_V_DIM = 128
POOL_WINDOWS = (2, 4, 8, 16)
POOL_GROUP = 256
POOL_HALO = 16
C_HEADS = 8
C_HEAD_DIM = 128
C_KV_HEADS = 2
IDX_HEADS = 8
IDX_DIM = 64
TOPK_MAX = 256

C_Q0, C_K0, C_V0, C_IQ0, C_IK0, C_IW0, C_COLS = 0, 1024, 1280, 1536, 2048, 2176, 2304

NEG = -1e30
LOG2E = math.log2(math.e)
V_ROWS = 128 + 16
VMEM_LIMIT = 56 * 1024 * 1024


def _cparams(sem):
    return pltpu.CompilerParams(dimension_semantics=sem, vmem_limit_bytes=VMEM_LIMIT)


def _rms_rows(x, g):
    ms = jnp.mean(x * x, axis=-1, keepdims=True)
    return (x * lax.rsqrt(ms + EPS)) * g


def _dot_nt(a, b):
    return lax.dot_general(a, b, (((1,), (1,)), ((), ())), preferred_element_type=F32)


def _proj_kernel(h_ref, g_ref, w_ref, gain_ref, seg_ref, o_ref, *rest, n_norm_tiles, seg, tn, aux):
    if aux:
        aux_ref, u_ref = rest
    else:
        (u_ref,) = rest
    j = pl.program_id(1)

    @pl.when(j == 0)
    def _():
        u_ref[...] = _rms_rows(h_ref[...], g_ref[...]).astype(BF16)

    acc = jnp.dot(u_ref[...], w_ref[...], preferred_element_type=F32)

    @pl.when(j < n_norm_tiles)
    def _():
        for c in range(tn // MXU_DIM):
            a = acc[:, c * MXU_DIM:(c + 1) * MXU_DIM]
            sq = a * a
            hi = sq.astype(BF16)
            lo = (sq - hi.astype(F32)).astype(BF16)
            ssum = (jnp.dot(hi, seg_ref[...], preferred_element_type=F32)
                    + jnp.dot(lo, seg_ref[...], preferred_element_type=F32))
            y = (a * lax.rsqrt(ssum * (1.0 / seg) + EPS)) * gain_ref[:, c * MXU_DIM:(c + 1) * MXU_DIM]
            o_ref[:, c * MXU_DIM:(c + 1) * MXU_DIM] = y.astype(BF16)

    @pl.when(j >= n_norm_tiles)
    def _():
        o_ref[...] = acc.astype(BF16)

    if aux:
        @pl.when(j == pl.num_programs(1) - 1)
        def _():
            aux_ref[...] = acc[:, tn - LANES:]


def _proj(h, g, w, gain, *, n_norm_cols, seg, aux):
    t, d = h.shape
    n = w.shape[1]
    tm = min(1024, t)
    tn = 512 if n % 512 == 0 else 256
    assert n % tn == 0 and n_norm_cols % tn == 0 and t % tm == 0
    r = lax.broadcasted_iota(jnp.int32, (MXU_DIM, MXU_DIM), 0) // seg
    c = lax.broadcasted_iota(jnp.int32, (MXU_DIM, MXU_DIM), 1) // seg
    segm = (r == c).astype(BF16)
    out_shape = [jax.ShapeDtypeStruct((t, n), BF16)]
    out_specs = [pl.BlockSpec((tm, tn), lambda i, j: (i, j))]
    if aux:
        out_shape.append(jax.ShapeDtypeStruct((t, LANES), F32))
        out_specs.append(pl.BlockSpec((tm, LANES), lambda i, j: (i, 0)))
    res = pl.pallas_call(
        functools.partial(_proj_kernel, n_norm_tiles=n_norm_cols // tn, seg=seg, tn=tn, aux=aux),
        grid=(t // tm, n // tn),
        in_specs=[
            pl.BlockSpec((tm, d), lambda i, j: (i, 0)),
            pl.BlockSpec((1, d), lambda i, j: (0, 0)),
            pl.BlockSpec((d, tn), lambda i, j: (0, j)),
            pl.BlockSpec((1, tn), lambda i, j: (0, j)),
            pl.BlockSpec((MXU_DIM, MXU_DIM), lambda i, j: (0, 0)),
        ],
        out_specs=out_specs,
        out_shape=out_shape,
        scratch_shapes=[pltpu.VMEM((tm, d), BF16)],
        compiler_params=_cparams(("parallel", "arbitrary")),
        name="proj",
    )(h, g, w, gain, segm)
    return res if aux else res[0]


def _outproj_kernel(o_ref, w_ref, h_ref, out_ref):
    out_ref[...] = h_ref[...] + jnp.dot(o_ref[...], w_ref[...], preferred_element_type=F32)


def _outproj(o, w, h):
    t, d = h.shape
    tm = min(1024, t)
    return pl.pallas_call(
        _outproj_kernel,
        grid=(t // tm,),
        in_specs=[
            pl.BlockSpec((tm, o.shape[1]), lambda i: (i, 0)),
            pl.BlockSpec(w.shape, lambda i: (0, 0)),
            pl.BlockSpec((tm, d), lambda i: (i, 0)),
        ],
        out_specs=pl.BlockSpec((tm, d), lambda i: (i, 0)),
        out_shape=jax.ShapeDtypeStruct((t, d), F32),
        compiler_params=_cparams(("parallel",)),
        name="outproj",
    )(o, w, h)


def _mlp_kernel(h_ref, g_ref, w1_ref, w2_ref, o_ref, u_ref, acc_ref):
    j = pl.program_id(1)

    @pl.when(j == 0)
    def _():
        u_ref[...] = _rms_rows(h_ref[...], g_ref[...]).astype(BF16)
        acc_ref[...] = jnp.zeros_like(acc_ref)

    hid = jnp.dot(u_ref[...], w1_ref[...], preferred_element_type=F32)
    hid = jnp.square(jnp.maximum(hid, 0.0)).astype(BF16)
    acc_ref[...] += jnp.dot(hid, w2_ref[...], preferred_element_type=F32)

    @pl.when(j == pl.num_programs(1) - 1)
    def _():
        o_ref[...] = h_ref[...] + acc_ref[...]


def _mlp(h, g, w1, w2):
    t, d = h.shape
    f = w1.shape[1]
    tm = min(1024, t)
    tf = 512
    return pl.pallas_call(
        _mlp_kernel,
        grid=(t // tm, f // tf),
        in_specs=[
            pl.BlockSpec((tm, d), lambda i, j: (i, 0)),
            pl.BlockSpec((1, d), lambda i, j: (0, 0)),
            pl.BlockSpec((d, tf), lambda i, j: (0, j)),
            pl.BlockSpec((tf, d), lambda i, j: (j, 0)),
        ],
        out_specs=pl.BlockSpec((tm, d), lambda i, j: (i, 0)),
        out_shape=jax.ShapeDtypeStruct((t, d), F32),
        scratch_shapes=[pltpu.VMEM((tm, d), BF16), pltpu.VMEM((tm, d), F32)],
        compiler_params=_cparams(("parallel", "arbitrary")),
        name="mlp",
    )(h, g, w1, w2)


def _softmax_step_t(st, vt1, m_ref, acc_ref):
    m_prev = m_ref[...]
    m_next = jnp.maximum(m_prev, jnp.max(st, axis=0, keepdims=True))
    p = jnp.exp2(st - m_next).astype(BF16)
    alpha = jnp.exp2(m_prev - m_next)
    m_ref[...] = m_next
    acc_ref[...] = alpha * acc_ref[...] + jnp.dot(vt1, p, preferred_element_type=F32)


def _pipelined_tiles(n, scores, consume, buf_a, buf_b):
    n_pairs = (n - 1) // 2
    scores(0, buf_a)

    def pair(t, c):
        scores(2 * t + 1, buf_b)
        consume(2 * t, buf_a, False)
        scores(2 * t + 2, buf_a)
        consume(2 * t + 1, buf_b, False)
        return c

    lax.fori_loop(0, n_pairs, pair, 0)

    @pl.when(n - 2 * n_pairs == 1)
    def _():
        consume(n - 1, buf_a, True)

    @pl.when(n - 2 * n_pairs == 2)
    def _():
        scores(n - 1, buf_b)
        consume(n - 2, buf_a, False)
        consume(n - 1, buf_b, True)


def _store_vt(vt_ref, v_ref, cols, seq, blk):
    for c in range(seq // blk):
        rows = slice(c * blk, (c + 1) * blk)
        vt_ref[0:LANES, rows] = v_ref[rows, cols].astype(F32).T.astype(BF16)
    vt_ref[LANES:, :] = jnp.ones((V_ROWS - LANES, seq), BF16)


def _attn_a_kernel(q_ref, k_ref, v_ref, lq1_ref, lk1_ref, lq2_ref, lk2_ref, subg_ref, o_ref,
                   vt_ref, m_ref, acc_ref, s0_ref, s1_ref, *, tq, tk, seq, lambda_init):
    i = pl.program_id(2)

    @pl.when(i == 0)
    def _():
        _store_vt(vt_ref, v_ref, slice(None), seq, tk)

    q = q_ref[...].astype(F32)
    lane = lax.broadcasted_iota(jnp.int32, (tq, LANES), 1)
    qm = jnp.concatenate([jnp.where(lane < A_HEAD_DIM, q, 0.0),
                          jnp.where(lane >= A_HEAD_DIM, q, 0.0)], axis=0).astype(BF16)
    m_ref[...] = jnp.full(m_ref.shape, NEG, F32)
    acc_ref[...] = jnp.zeros_like(acc_ref)

    def scores(j, buf):
        ks = pl.multiple_of(j * tk, tk)
        buf[...] = _dot_nt(k_ref[pl.ds(ks, tk), :], qm)

    def consume(j, buf, masked):
        ks = pl.multiple_of(j * tk, tk)
        st = buf[...]
        if masked:
            kpos = ks + lax.broadcasted_iota(jnp.int32, (tk, 2 * tq), 0)
            qpos = i * tq + (lax.broadcasted_iota(jnp.int32, (tk, 2 * tq), 1) & (tq - 1))
            st = jnp.where((kpos >> CHUNK_SHIFT) <= (qpos >> CHUNK_SHIFT), st, NEG)
        _softmax_step_t(st, vt_ref[:, pl.ds(ks, tk)], m_ref, acc_ref)

    _pipelined_tiles(((i + 1) * tq + tk - 1) // tk, scores, consume, s0_ref, s1_ref)

    acc = acc_ref[...]
    ot = acc[0:LANES] / acc[LANES:LANES + 1]
    lam = (jnp.exp(jnp.sum(lq1_ref[...] * lk1_ref[...], axis=1, keepdims=True))
           - jnp.exp(jnp.sum(lq2_ref[...] * lk2_ref[...], axis=1, keepdims=True)) + lambda_init)
    od = ot[:, :tq] - lam * ot[:, tq:]
    ms = jnp.mean(od * od, axis=0, keepdims=True)
    y = (od * lax.rsqrt(ms + EPS)) * subg_ref[...] * (1.0 - lambda_init)
    o_ref[...] = y.T.astype(BF16)


def _attn_a(qkv, lq1, lk1, lq2, lk2, subg, *, batch, seq, lambda_init):
    t = batch * seq
    tq = min(512, seq)
    tk = min(512, seq)
    nq = seq // tq
    vec = lambda n: pl.BlockSpec((1, n), lambda b, h, i: (0, 0))
    return pl.pallas_call(
        functools.partial(_attn_a_kernel, tq=tq, tk=tk, seq=seq, lambda_init=lambda_init),
        grid=(batch, A_HEADS, nq),
        in_specs=[
            pl.BlockSpec((tq, LANES), lambda b, h, i: (b * nq + i, h)),
            pl.BlockSpec((seq, LANES), lambda b, h, i: (b, A_HEADS + h)),
            pl.BlockSpec((seq, LANES), lambda b, h, i: (b, 2 * A_HEADS + h)),
            vec(A_HEAD_DIM), vec(A_HEAD_DIM), vec(A_HEAD_DIM), vec(A_HEAD_DIM),
            pl.BlockSpec((A_V_DIM, 1), lambda b, h, i: (0, 0)),
        ],
        out_specs=pl.BlockSpec((tq, LANES), lambda b, h, i: (b * nq + i, h)),
        out_shape=jax.ShapeDtypeStruct((t, A_HEADS * A_V_DIM), BF16),
        scratch_shapes=[pltpu.VMEM((V_ROWS, seq), BF16), pltpu.VMEM((1, 2 * tq), F32),
                        pltpu.VMEM((V_ROWS, 2 * tq), F32)] + [pltpu.VMEM((tk, 2 * tq), F32)] * 2,
        compiler_params=_cparams(("arbitrary", "arbitrary", "arbitrary")),
        name="attn_a",
    )(qkv, qkv, qkv, lq1, lk1, lq2, lk2, subg)


def _pool_kernel(h_ref, halo_ref, g_ref, wg_ref, sc_ref, o_ref, ext_ref, *, ts):
    i = pl.program_id(1)
    x = h_ref[...]
    u = _rms_rows(x, g_ref[...])
    uh = _rms_rows(halo_ref[...], g_ref[...])
    ext_ref[0:POOL_HALO, :] = jnp.where(i > 0, uh, 0.0)
    ext_ref[POOL_HALO:, :] = u
    pos = i * ts + lax.broadcasted_iota(jnp.int32, (ts, 1), 0)
    for g, w in enumerate(POOL_WINDOWS):
        cols = slice(g * POOL_GROUP, (g + 1) * POOL_GROUP)
        win = ext_ref[POOL_HALO:, cols]
        for d in range(1, w):
            win = win + ext_ref[pl.ds(POOL_HALO - d, ts), cols]
        count = jnp.minimum(pos + 1, w).astype(F32)
        pooled = win / count - u[:, cols]
        y = jnp.dot(pooled.astype(BF16), wg_ref[g], preferred_element_type=F32)
        o_ref[:, cols] = x[:, cols] + y * sc_ref[:, cols]


def _pool(h, g, wg, scale, *, batch, seq):
    t, d = h.shape
    ts = min(512, seq)
    ns = seq // ts
    hb = ts // POOL_HALO
    return pl.pallas_call(
        functools.partial(_pool_kernel, ts=ts),
        grid=(batch, ns),
        in_specs=[
            pl.BlockSpec((ts, d), lambda b, i: (b * ns + i, 0)),
            pl.BlockSpec((POOL_HALO, d), lambda b, i: (jnp.maximum((b * ns + i) * hb - 1, 0), 0)),
            pl.BlockSpec((1, d), lambda b, i: (0, 0)),
            pl.BlockSpec(wg.shape, lambda b, i: (0, 0, 0)),
            pl.BlockSpec((1, d), lambda b, i: (0, 0)),
        ],
        out_specs=pl.BlockSpec((ts, d), lambda b, i: (b * ns + i, 0)),
        out_shape=jax.ShapeDtypeStruct((t, d), F32),
        scratch_shapes=[pltpu.VMEM((ts + POOL_HALO, d), F32)],
        compiler_params=_cparams(("parallel", "arbitrary")),
        name="pool",
    )(h, h, g, wg, scale)


def _masked_heads(x, n_pairs, rows):
    xf = x.astype(F32)
    lane = lax.broadcasted_iota(jnp.int32, (rows, LANES), 1)
    out = []
    for p in range(n_pairs):
        xp = xf[:, p * LANES:(p + 1) * LANES]
        out.append(jnp.where(lane < IDX_DIM, xp, 0.0).astype(BF16))
        out.append(jnp.where(lane >= IDX_DIM, xp, 0.0).astype(BF16))
    return out


def _sum_sublane_groups(m, rows):
    parts = [m[r:r + 8] for r in range(0, rows, 8)]
    while len(parts) > 1:
        parts = [parts[a] + parts[a + 1] for a in range(0, len(parts), 2)]
    return parts[0]


def _dsa_kernel(q_ref, k_ref, v_ref, iq_ref, ik_ref, iw_ref, o_ref, sc_ref, vt_ref, m_ref, acc_ref,
                s0_ref, s1_ref, *, tq, kb, seq, topk, idx_bits, idx_scale):
    i = pl.program_id(1)
    rep = C_HEADS // C_KV_HEADS

    @pl.when(i == 0)
    def _():
        for g in range(C_KV_HEADS):
            _store_vt(vt_ref.at[g], v_ref, slice(g * C_HEAD_DIM, (g + 1) * C_HEAD_DIM), seq, kb)

    nblk = ((i + 1) * tq) // kb
    iw = iw_ref[0] * idx_scale
    iqh = _masked_heads(iq_ref[...], IDX_HEADS // 2, tq)
    qchunk = (i * tq + lax.broadcasted_iota(jnp.int32, (kb, tq), 1)) >> CHUNK_SHIFT
    krow = lax.broadcasted_iota(jnp.int32, (kb, tq), 0)

    def score_body(r, c):
        ks = pl.multiple_of(r * kb, kb)
        ikb = ik_ref[pl.ds(ks, kb), :]
        acc = jnp.zeros((kb, tq), F32)
        for h in range(IDX_HEADS):
            acc = acc + iw[h:h + 1, :] * jnp.maximum(_dot_nt(ikb, iqh[h]), 0.0)
        acc = jnp.where(((ks + krow) >> CHUNK_SHIFT) <= qchunk, acc, -jnp.inf)
        sc_ref[pl.ds(ks, kb), :] = acc
        return c

    lax.fori_loop(0, nblk, score_body, 0)

    n_acc = 4

    def count(pred):
        def body(r, accs):
            ks = pl.multiple_of(r * kb, kb)
            p = pred(sc_ref[pl.ds(ks, kb), :], ks)
            accs = list(accs)
            for g8 in range(kb // 8):
                a = accs[g8 % n_acc]
                accs[g8 % n_acc] = jnp.where(p[g8 * 8:(g8 + 1) * 8], a + 1.0, a)
            return tuple(accs)
        accs = lax.fori_loop(0, nblk, body, (jnp.zeros((8, tq), F32),) * n_acc)
        return jnp.sum(sum(accs), axis=0, keepdims=True)

    int_min = jnp.int32(-2 ** 31)

    def key_to_f32(key_u):
        ks_ = key_u ^ int_min
        bits = jnp.where(ks_ >= 0, ks_, ks_ ^ jnp.int32(0x7FFFFFFF))
        return lax.bitcast_convert_type(bits, F32)

    def bit_body(b, t_u):
        cand_u = t_u | lax.shift_left(jnp.int32(1), 31 - b)
        cand = key_to_f32(cand_u)
        cnt = count(lambda blk, ks: blk >= cand)
        return jnp.where(cnt >= topk, cand_u, t_u)

    t_u = lax.fori_loop(0, 32, bit_body, jnp.zeros((1, tq), jnp.int32))
    thr = key_to_f32(t_u)
    thr = jnp.where(thr != thr, -jnp.inf, thr)
    n_ge = count(lambda blk, ks: blk >= thr)

    def tie_cut():
        need = topk - count(lambda blk, ks: blk > thr)

        def idx_body(b, j_u):
            cand = j_u | lax.shift_left(jnp.int32(1), idx_bits - 1 - b)
            cnt = count(lambda blk, ks: (blk == thr) & ((ks + krow) < cand))
            return jnp.where(cnt < need, cand, j_u)

        return lax.fori_loop(0, idx_bits, idx_body, jnp.zeros((1, tq), jnp.int32))

    def no_cut():
        return jnp.full((1, tq), 2 ** idx_bits - 1, jnp.int32)

    jcut = lax.cond(jnp.max(n_ge) > topk, tie_cut, no_cut)
    jcut = jnp.where(thr == -jnp.inf, -1, jcut)

    m_ref[...] = jnp.full(m_ref.shape, NEG, F32)
    acc_ref[...] = jnp.zeros_like(acc_ref)
    qg = [jnp.concatenate([q_ref[:, (g * rep + e) * C_HEAD_DIM:(g * rep + e + 1) * C_HEAD_DIM]
                           for e in range(rep)], axis=0) for g in range(C_KV_HEADS)]

    def scores(r, buf):
        ks = pl.multiple_of(r * kb, kb)
        for g in range(C_KV_HEADS):
            buf[g] = _dot_nt(k_ref[pl.ds(ks, kb), g * C_HEAD_DIM:(g + 1) * C_HEAD_DIM], qg[g])

    def consume(r, buf, is_last):
        ks = pl.multiple_of(r * kb, kb)
        sc = sc_ref[pl.ds(ks, kb), :]
        sel = (sc > thr) | ((sc == thr) & ((ks + krow) <= jcut))
        bias = jnp.where(sel, 0.0, NEG)
        bias = jnp.concatenate([bias] * rep, axis=1)
        for g in range(C_KV_HEADS):
            _softmax_step_t(buf[g] + bias, vt_ref[g, :, pl.ds(ks, kb)], m_ref.at[g], acc_ref.at[g])

    _pipelined_tiles(nblk, scores, consume, s0_ref, s1_ref)
    for g in range(C_KV_HEADS):
        a = acc_ref[g]
        ot = a[0:LANES] / a[LANES:LANES + 1]
        for e in range(rep):
            hh = g * rep + e
            o_ref[:, hh * C_HEAD_DIM:(hh + 1) * C_HEAD_DIM] = ot[:, e * tq:(e + 1) * tq].T.astype(BF16)


def _dsa(proj, iw_t, *, batch, seq, topk):
    t = batch * seq
    tq = min(256, seq)
    nq = seq // tq
    kvw = C_KV_HEADS * C_HEAD_DIM
    iqw = IDX_HEADS * IDX_DIM
    return pl.pallas_call(
        functools.partial(_dsa_kernel, tq=tq, kb=tq, seq=seq, topk=float(topk), idx_bits=(seq - 1).bit_length(),
                          idx_scale=IDX_HEADS ** -0.5 * IDX_DIM ** -0.5),
        grid=(batch, nq),
        in_specs=[
            pl.BlockSpec((tq, C_HEADS * C_HEAD_DIM), lambda b, i: (b * nq + i, 0)),
            pl.BlockSpec((seq, kvw), lambda b, i: (b, C_K0 // kvw)),
            pl.BlockSpec((seq, kvw), lambda b, i: (b, C_V0 // kvw)),
            pl.BlockSpec((tq, iqw), lambda b, i: (b * nq + i, C_IQ0 // iqw)),
            pl.BlockSpec((seq, LANES), lambda b, i: (b, C_IK0 // LANES)),
            pl.BlockSpec((1, 8, tq), lambda b, i: (b, 0, i)),
        ],
        out_specs=pl.BlockSpec((tq, C_HEADS * C_HEAD_DIM), lambda b, i: (b * nq + i, 0)),
        out_shape=jax.ShapeDtypeStruct((t, C_HEADS * C_HEAD_DIM), BF16),
        scratch_shapes=[pltpu.VMEM((seq, tq), F32), pltpu.VMEM((C_KV_HEADS, V_ROWS, seq), BF16),
                        pltpu.VMEM((C_KV_HEADS, 1, C_HEADS // C_KV_HEADS * tq), F32),
                        pltpu.VMEM((C_KV_HEADS, V_ROWS, C_HEADS // C_KV_HEADS * tq), F32)]
                       + [pltpu.VMEM((C_KV_HEADS, tq, C_HEADS // C_KV_HEADS * tq), F32)] * 2,
        compiler_params=_cparams(("arbitrary", "arbitrary")),
        name="dsa",
    )(proj, proj, proj, proj, proj, iw_t)


def _pad_c_weight(w):
    iw = w[:, 2112:2120]
    ik = w[:, 2048:2112]
    pad = jnp.zeros((w.shape[0], C_COLS - C_IW0 - IDX_HEADS), w.dtype)
    return jnp.concatenate([w[:, :2048], ik, ik, iw, pad], axis=1).astype(BF16)


def kernel(x, norm1_g, norm2_g, a_w_in, a_q_norm_g, a_k_norm_g, a_lambda_q1, a_lambda_k1, a_lambda_q2,
           a_lambda_k2, a_subln_g, a_w_out, b_w_group, b_scale, c_w_in, c_q_norm_g, c_k_norm_g, c_w_out,
           mlp_w1, mlp_w2):
    batch, seq, d = x.shape
    depth = norm1_g.shape[0]
    h = x.reshape(batch * seq, d)
    row = lambda v: v.reshape(1, -1).astype(F32)
    for i in range(depth):
        m, j = i % N_MIXERS, i // N_MIXERS
        g1 = row(norm1_g[i])
        if m == 0:
            lambda_init = 0.8 - 0.6 * math.exp(-0.3 * i)
            gain = jnp.concatenate([jnp.tile(a_q_norm_g[j], 2 * A_HEADS) * (A_HEAD_DIM ** -0.5 * LOG2E),
                                    jnp.tile(a_k_norm_g[j], 2 * A_HEADS),
                                    jnp.ones((A_HEADS * A_V_DIM,), F32)]).reshape(1, -1)
            qkv = _proj(h, g1, a_w_in[j].astype(BF16), gain, n_norm_cols=2 * D_MODEL, seg=A_HEAD_DIM, aux=False)
            o = _attn_a(qkv, row(a_lambda_q1[j]), row(a_lambda_k1[j]), row(a_lambda_q2[j]), row(a_lambda_k2[j]),
                        a_subln_g[j].reshape(-1, 1).astype(F32), batch=batch, seq=seq, lambda_init=lambda_init)
            h = _outproj(o, a_w_out[j].astype(BF16), h)
        elif m == 1:
            h = _pool(h, g1, b_w_group[j].astype(BF16), row(b_scale[j]), batch=batch, seq=seq)
        else:
            gain = jnp.concatenate([jnp.tile(c_q_norm_g[j], C_HEADS) * (C_HEAD_DIM ** -0.5 * LOG2E),
                                    jnp.tile(c_k_norm_g[j], C_KV_HEADS),
                                    jnp.ones((C_COLS - C_V0,), F32)]).reshape(1, -1)
            proj, aux = _proj(h, g1, _pad_c_weight(c_w_in[j]), gain, n_norm_cols=C_V0, seg=C_HEAD_DIM, aux=True)
            iw_t = aux[:, :IDX_HEADS].reshape(batch, seq, IDX_HEADS).transpose(0, 2, 1)
            o = _dsa(proj, iw_t, batch=batch, seq=seq, topk=min(TOPK_MAX, seq // 4))
            h = _outproj(o, c_w_out[j].astype(BF16), h)
        h = _mlp(h, row(norm2_g[i]), mlp_w1[i].astype(BF16), mlp_w2[i].astype(BF16))
    return h.reshape(batch, seq, d)
```

```python
import functools
import math

import jax
import jax.numpy as jnp
from jax import lax
from jax.experimental import pallas as pl
from jax.experimental.pallas import tpu as pltpu

F32 = jnp.float32
BF16 = jnp.bfloat16

D_MODEL = 1024
N_MIXERS = 3
CHUNK = 64
CHUNK_SHIFT = 6
EPS = 1e-6
LANES = 128
MXU_DIM = 256

A_HEADS = 8
A_HEAD_DIM = 64
A_V_DIM = 128
POOL_WINDOWS = (2, 4, 8, 16)
POOL_GROUP = 256
POOL_HALO = 16
C_HEADS = 8
C_HEAD_DIM = 128
C_KV_HEADS = 2
IDX_HEADS = 8
IDX_DIM = 64
TOPK_MAX = 256

C_Q0, C_K0, C_V0, C_IQ0, C_IK0, C_IW0, C_COLS = 0, 1024, 1280, 1536, 2048, 2176, 2304

NEG = -1e30
LOG2E = math.log2(math.e)
V_ROWS = 128 + 16
VMEM_LIMIT = 56 * 1024 * 1024


def _cparams(sem):
    return pltpu.CompilerParams(dimension_semantics=sem, vmem_limit_bytes=VMEM_LIMIT)


def _rms_rows(x, g):
    ms = jnp.mean(x * x, axis=-1, keepdims=True)
    return (x * lax.rsqrt(ms + EPS)) * g


def _dot_nt(a, b):
    return lax.dot_general(a, b, (((1,), (1,)), ((), ())), preferred_element_type=F32)


def _proj_kernel(h_ref, g_ref, w_ref, gain_ref, seg_ref, o_ref, *aux_ref, n_norm_cols, seg):
    u = _rms_rows(h_ref[...], g_ref[...]).astype(BF16)
    n = w_ref.shape[1]
    for c0 in range(0, n, 2 * MXU_DIM):
        cw = min(2 * MXU_DIM, n - c0)
        acc = jnp.dot(u, w_ref[:, c0:c0 + cw], preferred_element_type=F32)
        for s0 in range(0, cw, MXU_DIM):
            cols = slice(c0 + s0, c0 + s0 + MXU_DIM)
            a = acc[:, s0:s0 + MXU_DIM]
            if c0 + s0 < n_norm_cols:
                sq = a * a
                hi = sq.astype(BF16)
                lo = (sq - hi.astype(F32)).astype(BF16)
                ssum = (jnp.dot(hi, seg_ref[...], preferred_element_type=F32)
                        + jnp.dot(lo, seg_ref[...], preferred_element_type=F32))
                a = (a * lax.rsqrt(ssum * (1.0 / seg) + EPS)) * gain_ref[:, cols]
            o_ref[:, cols] = a.astype(BF16)
        if aux_ref and c0 + cw == n:
            aux_ref[0][...] = acc[:, cw - LANES:]


def _proj(h, g, w, gain, *, n_norm_cols, seg, aux):
    t, d = h.shape
    n = w.shape[1]
    tm = min(512, t)
    assert n % MXU_DIM == 0 and n_norm_cols % MXU_DIM == 0 and t % tm == 0
    r = lax.broadcasted_iota(jnp.int32, (MXU_DIM, MXU_DIM), 0) // seg
    c = lax.broadcasted_iota(jnp.int32, (MXU_DIM, MXU_DIM), 1) // seg
    segm = (r == c).astype(BF16)
    out_shape = [jax.ShapeDtypeStruct((t, n), BF16)]
    out_specs = [pl.BlockSpec((tm, n), lambda i: (i, 0))]
    if aux:
        out_shape.append(jax.ShapeDtypeStruct((t, LANES), F32))
        out_specs.append(pl.BlockSpec((tm, LANES), lambda i: (i, 0)))
    res = pl.pallas_call(
        functools.partial(_proj_kernel, n_norm_cols=n_norm_cols, seg=seg),
        grid=(t // tm,),
        in_specs=[
            pl.BlockSpec((tm, d), lambda i: (i, 0)),
            pl.BlockSpec((1, d), lambda i: (0, 0)),
            pl.BlockSpec((d, n), lambda i: (0, 0)),
            pl.BlockSpec((1, n), lambda i: (0, 0)),
            pl.BlockSpec((MXU_DIM, MXU_DIM), lambda i: (0, 0)),
        ],
        out_specs=out_specs,
        out_shape=out_shape,
        compiler_params=_cparams(("parallel",)),
        name="proj",
    )(h, g, w, gain, segm)
    return res if aux else res[0]


def _outproj_kernel(o_ref, w_ref, h_ref, out_ref):
    out_ref[...] = h_ref[...] + jnp.dot(o_ref[...], w_ref[...], preferred_element_type=F32)


def _outproj(o, w, h):
    t, d = h.shape
    tm = min(1024, t)
    return pl.pallas_call(
        _outproj_kernel,
        grid=(t // tm,),
        in_specs=[
            pl.BlockSpec((tm, o.shape[1]), lambda i: (i, 0)),
            pl.BlockSpec(w.shape, lambda i: (0, 0)),
            pl.BlockSpec((tm, d), lambda i: (i, 0)),
        ],
        out_specs=pl.BlockSpec((tm, d), lambda i: (i, 0)),
        out_shape=jax.ShapeDtypeStruct((t, d), F32),
        compiler_params=_cparams(("parallel",)),
        name="outproj",
    )(o, w, h)


def _mlp_kernel(h_ref, g_ref, w1_ref, w2_ref, o_ref, u_ref, acc_ref):
    j = pl.program_id(1)

    @pl.when(j == 0)
    def _():
        u_ref[...] = _rms_rows(h_ref[...], g_ref[...]).astype(BF16)
        acc_ref[...] = jnp.zeros_like(acc_ref)

    hid = jnp.dot(u_ref[...], w1_ref[...], preferred_element_type=F32)
    hid = jnp.square(jnp.maximum(hid, 0.0)).astype(BF16)
    acc_ref[...] += jnp.dot(hid, w2_ref[...], preferred_element_type=F32)

    @pl.when(j == pl.num_programs(1) - 1)
    def _():
        o_ref[...] = h_ref[...] + acc_ref[...]


def _mlp(h, g, w1, w2):
    t, d = h.shape
    f = w1.shape[1]
    tm = min(1024, t)
    tf = 1024
    return pl.pallas_call(
        _mlp_kernel,
        grid=(t // tm, f // tf),
        in_specs=[
            pl.BlockSpec((tm, d), lambda i, j: (i, 0)),
            pl.BlockSpec((1, d), lambda i, j: (0, 0)),
            pl.BlockSpec((d, tf), lambda i, j: (0, j)),
            pl.BlockSpec((tf, d), lambda i, j: (j, 0)),
        ],
        out_specs=pl.BlockSpec((tm, d), lambda i, j: (i, 0)),
        out_shape=jax.ShapeDtypeStruct((t, d), F32),
        scratch_shapes=[pltpu.VMEM((tm, d), BF16), pltpu.VMEM((tm, d), F32)],
        compiler_params=_cparams(("parallel", "arbitrary")),
        name="mlp",
    )(h, g, w1, w2)


def _softmax_step_t(st, vt1, m_ref, acc_ref):
    m_prev = m_ref[...]
    m_next = jnp.maximum(m_prev, jnp.max(st, axis=0, keepdims=True))
    p = jnp.exp2(st - m_next).astype(BF16)
    alpha = jnp.exp2(m_prev - m_next)
    m_ref[...] = m_next
    acc_ref[...] = alpha * acc_ref[...] + jnp.dot(vt1, p, preferred_element_type=F32)


def _pipelined_tiles(n, scores, consume, buf_a, buf_b):
    n_pairs = (n - 1) // 2
    scores(0, buf_a)

    def pair(t, c):
        scores(2 * t + 1, buf_b)
        consume(2 * t, buf_a, False)
        scores(2 * t + 2, buf_a)
        consume(2 * t + 1, buf_b, False)
        return c

    lax.fori_loop(0, n_pairs, pair, 0)

    @pl.when(n - 2 * n_pairs == 1)
    def _():
        consume(n - 1, buf_a, True)

    @pl.when(n - 2 * n_pairs == 2)
    def _():
        scores(n - 1, buf_b)
        consume(n - 2, buf_a, False)
        consume(n - 1, buf_b, True)


def _store_vt(vt_ref, v_ref, cols, seq, blk):
    for c in range(seq // blk):
        rows = slice(c * blk, (c + 1) * blk)
        vt_ref[0:LANES, rows] = v_ref[rows, cols].astype(F32).T.astype(BF16)
    vt_ref[LANES:, :] = jnp.ones((V_ROWS - LANES, seq), BF16)


def _attn_a_kernel(q_ref, k_ref, v_ref, lq1_ref, lk1_ref, lq2_ref, lk2_ref, subg_ref, o_ref,
                   vt_ref, m_ref, acc_ref, s0_ref, s1_ref, *, tq, tk, seq, lambda_init):
    i = pl.program_id(2)

    @pl.when(i == 0)
    def _():
        _store_vt(vt_ref, v_ref, slice(None), seq, tk)

    q = q_ref[...].astype(F32)
    lane = lax.broadcasted_iota(jnp.int32, (tq, LANES), 1)
    qm = jnp.concatenate([jnp.where(lane < A_HEAD_DIM, q, 0.0),
                          jnp.where(lane >= A_HEAD_DIM, q, 0.0)], axis=0).astype(BF16)
    m_ref[...] = jnp.full(m_ref.shape, NEG, F32)
    acc_ref[...] = jnp.zeros_like(acc_ref)

    def scores(j, buf):
        ks = pl.multiple_of(j * tk, tk)
        buf[...] = _dot_nt(k_ref[pl.ds(ks, tk), :], qm)

    def consume(j, buf, masked):
        ks = pl.multiple_of(j * tk, tk)
        st = buf[...]
        if masked:
            kpos = ks + lax.broadcasted_iota(jnp.int32, (tk, 2 * tq), 0)
            qpos = i * tq + (lax.broadcasted_iota(jnp.int32, (tk, 2 * tq), 1) & (tq - 1))
            st = jnp.where((kpos >> CHUNK_SHIFT) <= (qpos >> CHUNK_SHIFT), st, NEG)
        _softmax_step_t(st, vt_ref[:, pl.ds(ks, tk)], m_ref, acc_ref)

    _pipelined_tiles(((i + 1) * tq + tk - 1) // tk, scores, consume, s0_ref, s1_ref)

    acc = acc_ref[...]
    ot = acc[0:LANES] / acc[LANES:LANES + 1]
    lam = (jnp.exp(jnp.sum(lq1_ref[...] * lk1_ref[...], axis=1, keepdims=True))
           - jnp.exp(jnp.sum(lq2_ref[...] * lk2_ref[...], axis=1, keepdims=True)) + lambda_init)
    od = ot[:, :tq] - lam * ot[:, tq:]
    ms = jnp.mean(od * od, axis=0, keepdims=True)
    y = (od * lax.rsqrt(ms + EPS)) * subg_ref[...] * (1.0 - lambda_init)
    o_ref[...] = y.T.astype(BF16)


def _attn_a(qkv, lq1, lk1, lq2, lk2, subg, *, batch, seq, lambda_init):
    t = batch * seq
    tq = min(512, seq)
    tk = min(512, seq)
    nq = seq // tq
    vec = lambda n: pl.BlockSpec((1, n), lambda b, h, i: (0, 0))
    return pl.pallas_call(
        functools.partial(_attn_a_kernel, tq=tq, tk=tk, seq=seq, lambda_init=lambda_init),
        grid=(batch, A_HEADS, nq),
        in_specs=[
            pl.BlockSpec((tq, LANES), lambda b, h, i: (b * nq + i, h)),
            pl.BlockSpec((seq, LANES), lambda b, h, i: (b, A_HEADS + h)),
            pl.BlockSpec((seq, LANES), lambda b, h, i: (b, 2 * A_HEADS + h)),
            vec(A_HEAD_DIM), vec(A_HEAD_DIM), vec(A_HEAD_DIM), vec(A_HEAD_DIM),
            pl.BlockSpec((A_V_DIM, 1), lambda b, h, i: (0, 0)),
        ],
        out_specs=pl.BlockSpec((tq, LANES), lambda b, h, i: (b * nq + i, h)),
        out_shape=jax.ShapeDtypeStruct((t, A_HEADS * A_V_DIM), BF16),
        scratch_shapes=[pltpu.VMEM((V_ROWS, seq), BF16), pltpu.VMEM((1, 2 * tq), F32),
                        pltpu.VMEM((V_ROWS, 2 * tq), F32)] + [pltpu.VMEM((tk, 2 * tq), F32)] * 2,
        compiler_params=_cparams(("arbitrary", "arbitrary", "arbitrary")),
        name="attn_a",
    )(qkv, qkv, qkv, lq1, lk1, lq2, lk2, subg)


def _pool_kernel(h_ref, halo_ref, g_ref, wg_ref, sc_ref, o_ref, ext_ref, *, ts):
    i = pl.program_id(1)
    x = h_ref[...]
    u = _rms_rows(x, g_ref[...])
    uh = _rms_rows(halo_ref[...], g_ref[...])
    ext_ref[0:POOL_HALO, :] = jnp.where(i > 0, uh, 0.0)
    ext_ref[POOL_HALO:, :] = u
    pos = i * ts + lax.broadcasted_iota(jnp.int32, (ts, 1), 0)
    for g, w in enumerate(POOL_WINDOWS):
        cols = slice(g * POOL_GROUP, (g + 1) * POOL_GROUP)
        win = ext_ref[POOL_HALO:, cols]
        for d in range(1, w):
            win = win + ext_ref[pl.ds(POOL_HALO - d, ts), cols]
        count = jnp.minimum(pos + 1, w).astype(F32)
        pooled = win / count - u[:, cols]
        y = jnp.dot(pooled.astype(BF16), wg_ref[g], preferred_element_type=F32)
        o_ref[:, cols] = x[:, cols] + y * sc_ref[:, cols]


def _pool(h, g, wg, scale, *, batch, seq):
    t, d = h.shape
    ts = min(512, seq)
    ns = seq // ts
    hb = ts // POOL_HALO
    return pl.pallas_call(
        functools.partial(_pool_kernel, ts=ts),
        grid=(batch, ns),
        in_specs=[
            pl.BlockSpec((ts, d), lambda b, i: (b * ns + i, 0)),
            pl.BlockSpec((POOL_HALO, d), lambda b, i: (jnp.maximum((b * ns + i) * hb - 1, 0), 0)),
            pl.BlockSpec((1, d), lambda b, i: (0, 0)),
            pl.BlockSpec(wg.shape, lambda b, i: (0, 0, 0)),
            pl.BlockSpec((1, d), lambda b, i: (0, 0)),
        ],
        out_specs=pl.BlockSpec((ts, d), lambda b, i: (b * ns + i, 0)),
        out_shape=jax.ShapeDtypeStruct((t, d), F32),
        scratch_shapes=[pltpu.VMEM((ts + POOL_HALO, d), F32)],
        compiler_params=_cparams(("parallel", "arbitrary")),
        name="pool",
    )(h, h, g, wg, scale)


def _masked_heads(x, n_pairs, rows):
    xf = x.astype(F32)
    lane = lax.broadcasted_iota(jnp.int32, (rows, LANES), 1)
    out = []
    for p in range(n_pairs):
        xp = xf[:, p * LANES:(p + 1) * LANES]
        out.append(jnp.where(lane < IDX_DIM, xp, 0.0).astype(BF16))
        out.append(jnp.where(lane >= IDX_DIM, xp, 0.0).astype(BF16))
    return out


def _sum_sublane_groups(m, rows):
    parts = [m[r:r + 8] for r in range(0, rows, 8)]
    while len(parts) > 1:
        parts = [parts[a] + parts[a + 1] for a in range(0, len(parts), 2)]
    return parts[0]


def _dsa_kernel(q_ref, k_ref, v_ref, iq_ref, ik_ref, iw_ref, o_ref, sc_ref, vt_ref, m_ref, acc_ref,
                s0_ref, s1_ref, *, tq, kb, seq, topk, idx_bits, idx_scale):
    i = pl.program_id(1)
    rep = C_HEADS // C_KV_HEADS

    @pl.when(i == 0)
    def _():
        for g in range(C_KV_HEADS):
            _store_vt(vt_ref.at[g], v_ref, slice(g * C_HEAD_DIM, (g + 1) * C_HEAD_DIM), seq, kb)

    nblk = ((i + 1) * tq) // kb
    iw = iw_ref[0] * idx_scale
    iqh = _masked_heads(iq_ref[...], IDX_HEADS // 2, tq)
    qchunk = (i * tq + lax.broadcasted_iota(jnp.int32, (kb, tq), 1)) >> CHUNK_SHIFT
    krow = lax.broadcasted_iota(jnp.int32, (kb, tq), 0)

    def score_body(r, c):
        ks = pl.multiple_of(r * kb, kb)
        ikb = ik_ref[pl.ds(ks, kb), :]
        acc = jnp.zeros((kb, tq), F32)
        for h in range(IDX_HEADS):
            acc = acc + iw[h:h + 1, :] * jnp.maximum(_dot_nt(ikb, iqh[h]), 0.0)
        acc = jnp.where(((ks + krow) >> CHUNK_SHIFT) <= qchunk, acc, -jnp.inf)
        sc_ref[pl.ds(ks, kb), :] = acc
        return c

    lax.fori_loop(0, nblk, score_body, 0)

    n_acc = 4

    def count(pred):
        def body(r, accs):
            ks = pl.multiple_of(r * kb, kb)
            p = pred(sc_ref[pl.ds(ks, kb), :], ks)
            accs = list(accs)
            for g8 in range(kb // 8):
                a = accs[g8 % n_acc]
                accs[g8 % n_acc] = jnp.where(p[g8 * 8:(g8 + 1) * 8], a + 1.0, a)
            return tuple(accs)
        accs = lax.fori_loop(0, nblk, body, (jnp.zeros((8, tq), F32),) * n_acc)
        return jnp.sum(sum(accs), axis=0, keepdims=True)

    int_min = jnp.int32(-2 ** 31)

    def key_to_f32(key_u):
        ks_ = key_u ^ int_min
        bits = jnp.where(ks_ >= 0, ks_, ks_ ^ jnp.int32(0x7FFFFFFF))
        return lax.bitcast_convert_type(bits, F32)

    def f32_to_key(x):
        bits = lax.bitcast_convert_type(x, jnp.int32)
        return jnp.where(bits >= 0, bits, bits ^ jnp.int32(0x7FFFFFFF)) ^ int_min

    def max_body(r, a):
        return jnp.maximum(a, sc_ref[pl.ds(pl.multiple_of(r * kb, kb), kb), :])

    colmax = lax.fori_loop(0, nblk, max_body, jnp.full((kb, tq), -jnp.inf, F32))
    hi_u = f32_to_key(jnp.max(colmax, axis=0, keepdims=True))
    lo_u = f32_to_key(jnp.min(colmax, axis=0, keepdims=True))
    n_known = jnp.min(lax.clz(lo_u ^ hi_u))
    known = jnp.where(n_known == 0, 0, lax.shift_left(jnp.int32(-1), jnp.minimum(32 - n_known, 31)))

    def bit_body(b, t_u):
        cand_u = t_u | lax.shift_left(jnp.int32(1), 31 - b)
        cand = key_to_f32(cand_u)
        cnt = count(lambda blk, ks: blk >= cand)
        return jnp.where(cnt >= topk, cand_u, t_u)

    t_u = lax.fori_loop(n_known, 32, bit_body, hi_u & known)
    n_adm = (((i * tq + lax.broadcasted_iota(jnp.int32, (1, tq), 1)) >> CHUNK_SHIFT) + 1) << CHUNK_SHIFT
    thr = jnp.where(n_adm < topk, -jnp.inf, key_to_f32(t_u))
    n_ge = count(lambda blk, ks: blk >= thr)

    def tie_cut():
        need = topk - count(lambda blk, ks: blk > thr)

        def idx_body(b, j_u):
            cand = j_u | lax.shift_left(jnp.int32(1), idx_bits - 1 - b)
            cnt = count(lambda blk, ks: (blk == thr) & ((ks + krow) < cand))
            return jnp.where(cnt < need, cand, j_u)

        return lax.fori_loop(0, idx_bits, idx_body, jnp.zeros((1, tq), jnp.int32))

    def no_cut():
        return jnp.full((1, tq), 2 ** idx_bits - 1, jnp.int32)

    jcut = lax.cond(jnp.max(n_ge) > topk, tie_cut, no_cut)
    jcut = jnp.where(thr == -jnp.inf, -1, jcut)

    m_ref[...] = jnp.full(m_ref.shape, NEG, F32)
    acc_ref[...] = jnp.zeros_like(acc_ref)
    qg = [jnp.concatenate([q_ref[:, (g * rep + e) * C_HEAD_DIM:(g * rep + e + 1) * C_HEAD_DIM]
                           for e in range(rep)], axis=0) for g in range(C_KV_HEADS)]

    def scores(r, buf):
        ks = pl.multiple_of(r * kb, kb)
        for g in range(C_KV_HEADS):
            buf[g] = _dot_nt(k_ref[pl.ds(ks, kb), g * C_HEAD_DIM:(g + 1) * C_HEAD_DIM], qg[g])

    def consume(r, buf, is_last):
        ks = pl.multiple_of(r * kb, kb)
        sc = sc_ref[pl.ds(ks, kb), :]
        sel = (sc > thr) | ((sc == thr) & ((ks + krow) <= jcut))
        bias = jnp.where(sel, 0.0, NEG)
        bias = jnp.concatenate([bias] * rep, axis=1)
        for g in range(C_KV_HEADS):
            _softmax_step_t(buf[g] + bias, vt_ref[g, :, pl.ds(ks, kb)], m_ref.at[g], acc_ref.at[g])

    _pipelined_tiles(nblk, scores, consume, s0_ref, s1_ref)
    for g in range(C_KV_HEADS):
        a = acc_ref[g]
        ot = a[0:LANES] / a[LANES:LANES + 1]
        for e in range(rep):
            hh = g * rep + e
            o_ref[:, hh * C_HEAD_DIM:(hh + 1) * C_HEAD_DIM] = ot[:, e * tq:(e + 1) * tq].T.astype(BF16)


def _dsa(proj, iw_t, *, batch, seq, topk):
    t = batch * seq
    tq = min(256, seq)
    nq = seq // tq
    kvw = C_KV_HEADS * C_HEAD_DIM
    iqw = IDX_HEADS * IDX_DIM
    return pl.pallas_call(
        functools.partial(_dsa_kernel, tq=tq, kb=tq, seq=seq, topk=float(topk), idx_bits=(seq - 1).bit_length(),
                          idx_scale=IDX_HEADS ** -0.5 * IDX_DIM ** -0.5),
        grid=(batch, nq),
        in_specs=[
            pl.BlockSpec((tq, C_HEADS * C_HEAD_DIM), lambda b, i: (b * nq + i, 0)),
            pl.BlockSpec((seq, kvw), lambda b, i: (b, C_K0 // kvw)),
            pl.BlockSpec((seq, kvw), lambda b, i: (b, C_V0 // kvw)),
            pl.BlockSpec((tq, iqw), lambda b, i: (b * nq + i, C_IQ0 // iqw)),
            pl.BlockSpec((seq, LANES), lambda b, i: (b, C_IK0 // LANES)),
            pl.BlockSpec((1, 8, tq), lambda b, i: (b, 0, i)),
        ],
        out_specs=pl.BlockSpec((tq, C_HEADS * C_HEAD_DIM), lambda b, i: (b * nq + i, 0)),
        out_shape=jax.ShapeDtypeStruct((t, C_HEADS * C_HEAD_DIM), BF16),
        scratch_shapes=[pltpu.VMEM((seq, tq), F32), pltpu.VMEM((C_KV_HEADS, V_ROWS, seq), BF16),
                        pltpu.VMEM((C_KV_HEADS, 1, C_HEADS // C_KV_HEADS * tq), F32),
                        pltpu.VMEM((C_KV_HEADS, V_ROWS, C_HEADS // C_KV_HEADS * tq), F32)]
                       + [pltpu.VMEM((C_KV_HEADS, tq, C_HEADS // C_KV_HEADS * tq), F32)] * 2,
        compiler_params=_cparams(("arbitrary", "arbitrary")),
        name="dsa",
    )(proj, proj, proj, proj, proj, iw_t)


def _pad_c_weight(w):
    iw = w[:, 2112:2120]
    ik = w[:, 2048:2112]
    pad = jnp.zeros((w.shape[0], C_COLS - C_IW0 - IDX_HEADS), w.dtype)
    return jnp.concatenate([w[:, :2048], ik, ik, iw, pad], axis=1).astype(BF16)


def kernel(x, norm1_g, norm2_g, a_w_in, a_q_norm_g, a_k_norm_g, a_lambda_q1, a_lambda_k1, a_lambda_q2,
           a_lambda_k2, a_subln_g, a_w_out, b_w_group, b_scale, c_w_in, c_q_norm_g, c_k_norm_g, c_w_out,
           mlp_w1, mlp_w2):
    batch, seq, d = x.shape
    depth = norm1_g.shape[0]
    h = x.reshape(batch * seq, d)
    row = lambda v: v.reshape(1, -1).astype(F32)
    for i in range(depth):
        m, j = i % N_MIXERS, i // N_MIXERS
        g1 = row(norm1_g[i])
        if m == 0:
            lambda_init = 0.8 - 0.6 * math.exp(-0.3 * i)
            gain = jnp.concatenate([jnp.tile(a_q_norm_g[j], 2 * A_HEADS) * (A_HEAD_DIM ** -0.5 * LOG2E),
                                    jnp.tile(a_k_norm_g[j], 2 * A_HEADS),
                                    jnp.ones((A_HEADS * A_V_DIM,), F32)]).reshape(1, -1)
            qkv = _proj(h, g1, a_w_in[j].astype(BF16), gain, n_norm_cols=2 * D_MODEL, seg=A_HEAD_DIM, aux=False)
            o = _attn_a(qkv, row(a_lambda_q1[j]), row(a_lambda_k1[j]), row(a_lambda_q2[j]), row(a_lambda_k2[j]),
                        a_subln_g[j].reshape(-1, 1).astype(F32), batch=batch, seq=seq, lambda_init=lambda_init)
            h = _outproj(o, a_w_out[j].astype(BF16), h)
        elif m == 1:
            h = _pool(h, g1, b_w_group[j].astype(BF16), row(b_scale[j]), batch=batch, seq=seq)
        else:
            gain = jnp.concatenate([jnp.tile(c_q_norm_g[j], C_HEADS) * (C_HEAD_DIM ** -0.5 * LOG2E),
                                    jnp.tile(c_k_norm_g[j], C_KV_HEADS),
                                    jnp.ones((C_COLS - C_V0,), F32)]).reshape(1, -1)
            proj, aux = _proj(h, g1, _pad_c_weight(c_w_in[j]), gain, n_norm_cols=C_V0, seg=C_HEAD_DIM, aux=True)
            iw_t = aux[:, :IDX_HEADS].reshape(batch, seq, IDX_HEADS).transpose(0, 2, 1)
            o = _dsa(proj, iw_t, batch=batch, seq=seq, topk=min(TOPK_MAX, seq // 4))
            h = _outproj(o, c_w_out[j].astype(BF16), h)
        h = _mlp(h, row(norm2_g[i]), mlp_w1[i].astype(BF16), mlp_w2[i].astype(BF16))
    return h.reshape(batch, seq, d)
```

```python
import functools
import math

import jax
import jax.numpy as jnp
from jax import lax
from jax.experimental import pallas as pl
from jax.experimental.pallas import tpu as pltpu

F32 = jnp.float32
BF16 = jnp.bfloat16

D_MODEL = 1024
N_MIXERS = 3
CHUNK = 64
CHUNK_SHIFT = 6
EPS = 1e-6
LANES = 128
MXU_DIM = 256

A_HEADS = 8
A_HEAD_DIM = 64
A_V_DIM = 128
POOL_WINDOWS = (2, 4, 8, 16)
POOL_GROUP = 256
POOL_HALO = 16
C_HEADS = 8
C_HEAD_DIM = 128
C_KV_HEADS = 2
IDX_HEADS = 8
IDX_DIM = 64
TOPK_MAX = 256

C_Q0, C_K0, C_V0, C_IQ0, C_IK0, C_IW0, C_COLS = 0, 1024, 1280, 1536, 2048, 2176, 2304

NEG = -1e30
LOG2E = math.log2(math.e)
V_ROWS = 128 + 16
VMEM_LIMIT = 56 * 1024 * 1024


def _cparams(sem):
    return pltpu.CompilerParams(dimension_semantics=sem, vmem_limit_bytes=VMEM_LIMIT)


def _rms_rows(x, g):
    ms = jnp.mean(x * x, axis=-1, keepdims=True)
    return (x * lax.rsqrt(ms + EPS)) * g


def _dot_nt(a, b):
    return lax.dot_general(a, b, (((1,), (1,)), ((), ())), preferred_element_type=F32)


def _proj_kernel(h_ref, g_ref, w_ref, gain_ref, seg_ref, o_ref, *aux_ref, n_norm_cols, seg):
    u = _rms_rows(h_ref[...], g_ref[...]).astype(BF16)
    n = w_ref.shape[1]
    for c0 in range(0, n, 2 * MXU_DIM):
        cw = min(2 * MXU_DIM, n - c0)
        acc = jnp.dot(u, w_ref[:, c0:c0 + cw], preferred_element_type=F32)
        for s0 in range(0, cw, MXU_DIM):
            cols = slice(c0 + s0, c0 + s0 + MXU_DIM)
            a = acc[:, s0:s0 + MXU_DIM]
            if c0 + s0 < n_norm_cols:
                sq = a * a
                hi = sq.astype(BF16)
                lo = (sq - hi.astype(F32)).astype(BF16)
                ssum = (jnp.dot(hi, seg_ref[...], preferred_element_type=F32)
                        + jnp.dot(lo, seg_ref[...], preferred_element_type=F32))
                a = (a * lax.rsqrt(ssum * (1.0 / seg) + EPS)) * gain_ref[:, cols]
            o_ref[:, cols] = a.astype(BF16)
        if aux_ref and c0 + cw == n:
            aux_ref[0][...] = acc[:, cw - LANES:]


def _proj(h, g, w, gain, *, n_norm_cols, seg, aux):
    t, d = h.shape
    n = w.shape[1]
    tm = min(512, t)
    assert n % MXU_DIM == 0 and n_norm_cols % MXU_DIM == 0 and t % tm == 0
    r = lax.broadcasted_iota(jnp.int32, (MXU_DIM, MXU_DIM), 0) // seg
    c = lax.broadcasted_iota(jnp.int32, (MXU_DIM, MXU_DIM), 1) // seg
    segm = (r == c).astype(BF16)
    out_shape = [jax.ShapeDtypeStruct((t, n), BF16)]
    out_specs = [pl.BlockSpec((tm, n), lambda i: (i, 0))]
    if aux:
        out_shape.append(jax.ShapeDtypeStruct((t, LANES), F32))
        out_specs.append(pl.BlockSpec((tm, LANES), lambda i: (i, 0)))
    res = pl.pallas_call(
        functools.partial(_proj_kernel, n_norm_cols=n_norm_cols, seg=seg),
        grid=(t // tm,),
        in_specs=[
            pl.BlockSpec((tm, d), lambda i: (i, 0)),
            pl.BlockSpec((1, d), lambda i: (0, 0)),
            pl.BlockSpec((d, n), lambda i: (0, 0)),
            pl.BlockSpec((1, n), lambda i: (0, 0)),
            pl.BlockSpec((MXU_DIM, MXU_DIM), lambda i: (0, 0)),
        ],
        out_specs=out_specs,
        out_shape=out_shape,
        compiler_params=_cparams(("parallel",)),
        name="proj",
    )(h, g, w, gain, segm)
    return res if aux else res[0]


def _outproj_kernel(o_ref, w_ref, h_ref, out_ref):
    out_ref[...] = h_ref[...] + jnp.dot(o_ref[...], w_ref[...], preferred_element_type=F32)


def _outproj(o, w, h):
    t, d = h.shape
    tm = min(1024, t)
    return pl.pallas_call(
        _outproj_kernel,
        grid=(t // tm,),
        in_specs=[
            pl.BlockSpec((tm, o.shape[1]), lambda i: (i, 0)),
            pl.BlockSpec(w.shape, lambda i: (0, 0)),
            pl.BlockSpec((tm, d), lambda i: (i, 0)),
        ],
        out_specs=pl.BlockSpec((tm, d), lambda i: (i, 0)),
        out_shape=jax.ShapeDtypeStruct((t, d), F32),
        compiler_params=_cparams(("parallel",)),
        name="outproj",
    )(o, w, h)


def _mlp_kernel(h_ref, g_ref, w1_ref, w2_ref, o_ref, u_ref, acc_ref):
    j = pl.program_id(1)

    @pl.when(j == 0)
    def _():
        u_ref[...] = _rms_rows(h_ref[...], g_ref[...]).astype(BF16)
        acc_ref[...] = jnp.zeros_like(acc_ref)

    hid = jnp.dot(u_ref[...], w1_ref[...].astype(BF16), preferred_element_type=F32)
    hid = jnp.square(jnp.maximum(hid, 0.0)).astype(BF16)
    acc_ref[...] += jnp.dot(hid, w2_ref[...].astype(BF16), preferred_element_type=F32)

    @pl.when(j == pl.num_programs(1) - 1)
    def _():
        o_ref[...] = h_ref[...] + acc_ref[...]


def _mlp(h, g, w1, w2):
    t, d = h.shape
    f = w1.shape[1]
    tm = min(1024, t)
    tf = 1024
    return pl.pallas_call(
        _mlp_kernel,
        grid=(t // tm, f // tf),
        in_specs=[
            pl.BlockSpec((tm, d), lambda i, j: (i, 0)),
            pl.BlockSpec((1, d), lambda i, j: (0, 0)),
            pl.BlockSpec((d, tf), lambda i, j: (0, j)),
            pl.BlockSpec((tf, d), lambda i, j: (j, 0)),
        ],
        out_specs=pl.BlockSpec((tm, d), lambda i, j: (i, 0)),
        out_shape=jax.ShapeDtypeStruct((t, d), F32),
        scratch_shapes=[pltpu.VMEM((tm, d), BF16), pltpu.VMEM((tm, d), F32)],
        compiler_params=_cparams(("parallel", "arbitrary")),
        name="mlp",
    )(h, g, w1, w2)


def _softmax_step_t(st, vt1, m_ref, acc_ref):
    m_prev = m_ref[...]
    m_next = jnp.maximum(m_prev, jnp.max(st, axis=0, keepdims=True))
    p = jnp.exp2(st - m_next).astype(BF16)
    alpha = jnp.exp2(m_prev - m_next)
    m_ref[...] = m_next
    acc_ref[...] = alpha * acc_ref[...] + jnp.dot(vt1, p, preferred_element_type=F32)


def _pipelined_tiles(n, scores, consume, buf_a, buf_b):
    n_pairs = (n - 1) // 2
    scores(0, buf_a)

    def pair(t, c):
        scores(2 * t + 1, buf_b)
        consume(2 * t, buf_a, False)
        scores(2 * t + 2, buf_a)
        consume(2 * t + 1, buf_b, False)
        return c

    lax.fori_loop(0, n_pairs, pair, 0)

    @pl.when(n - 2 * n_pairs == 1)
    def _():
        consume(n - 1, buf_a, True)

    @pl.when(n - 2 * n_pairs == 2)
    def _():
        scores(n - 1, buf_b)
        consume(n - 2, buf_a, False)
        consume(n - 1, buf_b, True)


def _store_vt(vt_ref, v_ref, cols, seq, blk):
    for c in range(seq // blk):
        rows = slice(c * blk, (c + 1) * blk)
        vt_ref[0:LANES, rows] = v_ref[rows, cols].astype(F32).T.astype(BF16)
    vt_ref[LANES:, :] = jnp.ones((V_ROWS - LANES, seq), BF16)


A_HEADS_PER_STEP = 2


def _attn_a_kernel(q_ref, k_ref, v_ref, lq1_ref, lk1_ref, lq2_ref, lk2_ref, subg_ref, o_ref,
                   vt_ref, m_ref, acc_ref, s0_ref, s1_ref, *, tq, tk, seq, lambda_init):
    i = pl.program_id(2)
    heads = range(A_HEADS_PER_STEP)
    hcols = lambda hh: slice(hh * LANES, (hh + 1) * LANES)

    @pl.when(i == 0)
    def _():
        for hh in heads:
            _store_vt(vt_ref.at[hh], v_ref, hcols(hh), seq, tk)

    lane = lax.broadcasted_iota(jnp.int32, (tq, LANES), 1)
    qm = []
    for hh in heads:
        q = q_ref[:, hcols(hh)].astype(F32)
        qm.append(jnp.concatenate([jnp.where(lane < A_HEAD_DIM, q, 0.0),
                                   jnp.where(lane >= A_HEAD_DIM, q, 0.0)], axis=0).astype(BF16))
    m_ref[...] = jnp.full(m_ref.shape, NEG, F32)
    acc_ref[...] = jnp.zeros_like(acc_ref)

    def scores(j, buf):
        ks = pl.multiple_of(j * tk, tk)
        for hh in heads:
            buf[hh] = _dot_nt(k_ref[pl.ds(ks, tk), hcols(hh)], qm[hh])

    def consume(j, buf, masked):
        ks = pl.multiple_of(j * tk, tk)
        if masked:
            kpos = ks + lax.broadcasted_iota(jnp.int32, (tk, 2 * tq), 0)
            qpos = i * tq + (lax.broadcasted_iota(jnp.int32, (tk, 2 * tq), 1) & (tq - 1))
            ok = (kpos >> CHUNK_SHIFT) <= (qpos >> CHUNK_SHIFT)
        for hh in heads:
            st = buf[hh]
            if masked:
                st = jnp.where(ok, st, NEG)
            _softmax_step_t(st, vt_ref[hh, :, pl.ds(ks, tk)], m_ref.at[hh], acc_ref.at[hh])

    _pipelined_tiles(((i + 1) * tq + tk - 1) // tk, scores, consume, s0_ref, s1_ref)

    lam = (jnp.exp(jnp.sum(lq1_ref[...] * lk1_ref[...], axis=1, keepdims=True))
           - jnp.exp(jnp.sum(lq2_ref[...] * lk2_ref[...], axis=1, keepdims=True)) + lambda_init)
    for hh in heads:
        acc = acc_ref[hh]
        ot = acc[0:LANES] / acc[LANES:LANES + 1]
        od = ot[:, :tq] - lam * ot[:, tq:]
        ms = jnp.mean(od * od, axis=0, keepdims=True)
        y = (od * lax.rsqrt(ms + EPS)) * subg_ref[...] * (1.0 - lambda_init)
        o_ref[:, hcols(hh)] = y.T.astype(BF16)


def _attn_a(qkv, lq1, lk1, lq2, lk2, subg, *, batch, seq, lambda_init):
    t = batch * seq
    tq = min(512, seq)
    tk = min(512, seq)
    nq = seq // tq
    hp = A_HEADS_PER_STEP
    ngrp = A_HEADS // hp
    vec = lambda n: pl.BlockSpec((1, n), lambda b, h, i: (0, 0))
    return pl.pallas_call(
        functools.partial(_attn_a_kernel, tq=tq, tk=tk, seq=seq, lambda_init=lambda_init),
        grid=(batch, ngrp, nq),
        in_specs=[
            pl.BlockSpec((tq, hp * LANES), lambda b, h, i: (b * nq + i, h)),
            pl.BlockSpec((seq, hp * LANES), lambda b, h, i: (b, ngrp + h)),
            pl.BlockSpec((seq, hp * LANES), lambda b, h, i: (b, 2 * ngrp + h)),
            vec(A_HEAD_DIM), vec(A_HEAD_DIM), vec(A_HEAD_DIM), vec(A_HEAD_DIM),
            pl.BlockSpec((A_V_DIM, 1), lambda b, h, i: (0, 0)),
        ],
        out_specs=pl.BlockSpec((tq, hp * LANES), lambda b, h, i: (b * nq + i, h)),
        out_shape=jax.ShapeDtypeStruct((t, A_HEADS * A_V_DIM), BF16),
        scratch_shapes=[pltpu.VMEM((hp, V_ROWS, seq), BF16), pltpu.VMEM((hp, 1, 2 * tq), F32),
                        pltpu.VMEM((hp, V_ROWS, 2 * tq), F32)] + [pltpu.VMEM((hp, tk, 2 * tq), F32)] * 2,
        compiler_params=_cparams(("arbitrary", "arbitrary", "arbitrary")),
        name="attn_a",
    )(qkv, qkv, qkv, lq1, lk1, lq2, lk2, subg)


def _pool_kernel(h_ref, halo_ref, g_ref, wg_ref, sc_ref, o_ref, ext_ref, *, ts):
    i = pl.program_id(1)
    x = h_ref[...]
    u = _rms_rows(x, g_ref[...])
    uh = _rms_rows(halo_ref[...], g_ref[...])
    ext_ref[0:POOL_HALO, :] = jnp.where(i > 0, uh, 0.0)
    ext_ref[POOL_HALO:, :] = u
    pos = i * ts + lax.broadcasted_iota(jnp.int32, (ts, 1), 0)
    for g, w in enumerate(POOL_WINDOWS):
        cols = slice(g * POOL_GROUP, (g + 1) * POOL_GROUP)
        win = ext_ref[POOL_HALO:, cols]
        for d in range(1, w):
            win = win + ext_ref[pl.ds(POOL_HALO - d, ts), cols]
        count = jnp.minimum(pos + 1, w).astype(F32)
        pooled = win / count - u[:, cols]
        y = jnp.dot(pooled.astype(BF16), wg_ref[g], preferred_element_type=F32)
        o_ref[:, cols] = x[:, cols] + y * sc_ref[:, cols]


def _pool(h, g, wg, scale, *, batch, seq):
    t, d = h.shape
    ts = min(512, seq)
    ns = seq // ts
    hb = ts // POOL_HALO
    return pl.pallas_call(
        functools.partial(_pool_kernel, ts=ts),
        grid=(batch, ns),
        in_specs=[
            pl.BlockSpec((ts, d), lambda b, i: (b * ns + i, 0)),
            pl.BlockSpec((POOL_HALO, d), lambda b, i: (jnp.maximum((b * ns + i) * hb - 1, 0), 0)),
            pl.BlockSpec((1, d), lambda b, i: (0, 0)),
            pl.BlockSpec(wg.shape, lambda b, i: (0, 0, 0)),
            pl.BlockSpec((1, d), lambda b, i: (0, 0)),
        ],
        out_specs=pl.BlockSpec((ts, d), lambda b, i: (b * ns + i, 0)),
        out_shape=jax.ShapeDtypeStruct((t, d), F32),
        scratch_shapes=[pltpu.VMEM((ts + POOL_HALO, d), F32)],
        compiler_params=_cparams(("parallel", "arbitrary")),
        name="pool",
    )(h, h, g, wg, scale)


def _masked_heads(x, n_pairs, rows):
    xf = x.astype(F32)
    lane = lax.broadcasted_iota(jnp.int32, (rows, LANES), 1)
    out = []
    for p in range(n_pairs):
        xp = xf[:, p * LANES:(p + 1) * LANES]
        out.append(jnp.where(lane < IDX_DIM, xp, 0.0).astype(BF16))
        out.append(jnp.where(lane >= IDX_DIM, xp, 0.0).astype(BF16))
    return out


def _sum_sublane_groups(m, rows):
    parts = [m[r:r + 8] for r in range(0, rows, 8)]
    while len(parts) > 1:
        parts = [parts[a] + parts[a + 1] for a in range(0, len(parts), 2)]
    return parts[0]


def _dsa_kernel(q_ref, k_ref, v_ref, iq_ref, ik_ref, iw_ref, o_ref, sc_ref, vt_ref, m_ref, acc_ref,
                s0_ref, s1_ref, *, tq, kb, seq, topk, idx_bits, idx_scale):
    i = pl.program_id(1)
    rep = C_HEADS // C_KV_HEADS

    @pl.when(i == 0)
    def _():
        for g in range(C_KV_HEADS):
            _store_vt(vt_ref.at[g], v_ref, slice(g * C_HEAD_DIM, (g + 1) * C_HEAD_DIM), seq, kb)

    nblk = ((i + 1) * tq) // kb
    iw = iw_ref[0] * idx_scale
    iqh = _masked_heads(iq_ref[...], IDX_HEADS // 2, tq)
    qchunk = (i * tq + lax.broadcasted_iota(jnp.int32, (kb, tq), 1)) >> CHUNK_SHIFT
    krow = lax.broadcasted_iota(jnp.int32, (kb, tq), 0)

    def score_body(r, c):
        ks = pl.multiple_of(r * kb, kb)
        ikb = ik_ref[pl.ds(ks, kb), :]
        acc = jnp.zeros((kb, tq), F32)
        for h in range(IDX_HEADS):
            acc = acc + iw[h:h + 1, :] * jnp.maximum(_dot_nt(ikb, iqh[h]), 0.0)
        acc = jnp.where(((ks + krow) >> CHUNK_SHIFT) <= qchunk, acc, -jnp.inf)
        sc_ref[pl.ds(ks, kb), :] = acc
        return c

    lax.fori_loop(0, nblk, score_body, 0)

    n_acc = 4

    def count(pred):
        def body(r, accs):
            ks = pl.multiple_of(r * kb, kb)
            p = pred(sc_ref[pl.ds(ks, kb), :], ks)
            accs = list(accs)
            for g8 in range(kb // 8):
                a = accs[g8 % n_acc]
                accs[g8 % n_acc] = jnp.where(p[g8 * 8:(g8 + 1) * 8], a + 1.0, a)
            return tuple(accs)
        accs = lax.fori_loop(0, nblk, body, (jnp.zeros((8, tq), F32),) * n_acc)
        return jnp.sum(sum(accs), axis=0, keepdims=True)

    int_min = jnp.int32(-2 ** 31)

    def key_to_f32(key_u):
        ks_ = key_u ^ int_min
        bits = jnp.where(ks_ >= 0, ks_, ks_ ^ jnp.int32(0x7FFFFFFF))
        return lax.bitcast_convert_type(bits, F32)

    def bit_body(b, t_u):
        cand_u = t_u | lax.shift_left(jnp.int32(1), 31 - b)
        cand = key_to_f32(cand_u)
        cnt = count(lambda blk, ks: blk >= cand)
        return jnp.where(cnt >= topk, cand_u, t_u)

    t_u = lax.fori_loop(0, 32, bit_body, jnp.zeros((1, tq), jnp.int32))
    n_adm = (((i * tq + lax.broadcasted_iota(jnp.int32, (1, tq), 1)) >> CHUNK_SHIFT) + 1) << CHUNK_SHIFT
    thr = jnp.where(n_adm < topk, -jnp.inf, key_to_f32(t_u))
    n_ge = count(lambda blk, ks: blk >= thr)

    def tie_cut():
        need = topk - count(lambda blk, ks: blk > thr)

        def idx_body(b, j_u):
            cand = j_u | lax.shift_left(jnp.int32(1), idx_bits - 1 - b)
            cnt = count(lambda blk, ks: (blk == thr) & ((ks + krow) < cand))
            return jnp.where(cnt < need, cand, j_u)

        return lax.fori_loop(0, idx_bits, idx_body, jnp.zeros((1, tq), jnp.int32))

    def no_cut():
        return jnp.full((1, tq), 2 ** idx_bits - 1, jnp.int32)

    jcut = lax.cond(jnp.max(n_ge) > topk, tie_cut, no_cut)
    jcut = jnp.where(thr == -jnp.inf, -1, jcut)

    m_ref[...] = jnp.full(m_ref.shape, NEG, F32)
    acc_ref[...] = jnp.zeros_like(acc_ref)
    qg = [jnp.concatenate([q_ref[:, (g * rep + e) * C_HEAD_DIM:(g * rep + e + 1) * C_HEAD_DIM]
                           for e in range(rep)], axis=0) for g in range(C_KV_HEADS)]

    def scores(r, buf):
        ks = pl.multiple_of(r * kb, kb)
        for g in range(C_KV_HEADS):
            buf[g] = _dot_nt(k_ref[pl.ds(ks, kb), g * C_HEAD_DIM:(g + 1) * C_HEAD_DIM], qg[g])

    def consume(r, buf, is_last):
        ks = pl.multiple_of(r * kb, kb)
        sc = sc_ref[pl.ds(ks, kb), :]
        sel = (sc > thr) | ((sc == thr) & ((ks + krow) <= jcut))
        bias = jnp.where(sel, 0.0, NEG)
        bias = jnp.concatenate([bias] * rep, axis=1)
        for g in range(C_KV_HEADS):
            _softmax_step_t(buf[g] + bias, vt_ref[g, :, pl.ds(ks, kb)], m_ref.at[g], acc_ref.at[g])

    _pipelined_tiles(nblk, scores, consume, s0_ref, s1_ref)
    for g in range(C_KV_HEADS):
        a = acc_ref[g]
        ot = a[0:LANES] / a[LANES:LANES + 1]
        for e in range(rep):
            hh = g * rep + e
            o_ref[:, hh * C_HEAD_DIM:(hh + 1) * C_HEAD_DIM] = ot[:, e * tq:(e + 1) * tq].T.astype(BF16)


def _dsa(proj, iw_t, *, batch, seq, topk):
    t = batch * seq
    tq = min(256, seq)
    nq = seq // tq
    kvw = C_KV_HEADS * C_HEAD_DIM
    iqw = IDX_HEADS * IDX_DIM
    return pl.pallas_call(
        functools.partial(_dsa_kernel, tq=tq, kb=tq, seq=seq, topk=float(topk), idx_bits=(seq - 1).bit_length(),
                          idx_scale=IDX_HEADS ** -0.5 * IDX_DIM ** -0.5),
        grid=(batch, nq),
        in_specs=[
            pl.BlockSpec((tq, C_HEADS * C_HEAD_DIM), lambda b, i: (b * nq + i, 0)),
            pl.BlockSpec((seq, kvw), lambda b, i: (b, C_K0 // kvw)),
            pl.BlockSpec((seq, kvw), lambda b, i: (b, C_V0 // kvw)),
            pl.BlockSpec((tq, iqw), lambda b, i: (b * nq + i, C_IQ0 // iqw)),
            pl.BlockSpec((seq, LANES), lambda b, i: (b, C_IK0 // LANES)),
            pl.BlockSpec((1, 8, tq), lambda b, i: (b, 0, i)),
        ],
        out_specs=pl.BlockSpec((tq, C_HEADS * C_HEAD_DIM), lambda b, i: (b * nq + i, 0)),
        out_shape=jax.ShapeDtypeStruct((t, C_HEADS * C_HEAD_DIM), BF16),
        scratch_shapes=[pltpu.VMEM((seq, tq), F32), pltpu.VMEM((C_KV_HEADS, V_ROWS, seq), BF16),
                        pltpu.VMEM((C_KV_HEADS, 1, C_HEADS // C_KV_HEADS * tq), F32),
                        pltpu.VMEM((C_KV_HEADS, V_ROWS, C_HEADS // C_KV_HEADS * tq), F32)]
                       + [pltpu.VMEM((C_KV_HEADS, tq, C_HEADS // C_KV_HEADS * tq), F32)] * 2,
        compiler_params=_cparams(("arbitrary", "arbitrary")),
        name="dsa",
    )(proj, proj, proj, proj, proj, iw_t)


def _pad_c_weight(w):
    iw = w[:, 2112:2120]
    ik = w[:, 2048:2112]
    pad = jnp.zeros((w.shape[0], C_COLS - C_IW0 - IDX_HEADS), w.dtype)
    return jnp.concatenate([w[:, :2048], ik, ik, iw, pad], axis=1).astype(BF16)


def kernel(x, norm1_g, norm2_g, a_w_in, a_q_norm_g, a_k_norm_g, a_lambda_q1, a_lambda_k1, a_lambda_q2,
           a_lambda_k2, a_subln_g, a_w_out, b_w_group, b_scale, c_w_in, c_q_norm_g, c_k_norm_g, c_w_out,
           mlp_w1, mlp_w2):
    batch, seq, d = x.shape
    depth = norm1_g.shape[0]
    h = x.reshape(batch * seq, d)
    row = lambda v: v.reshape(1, -1).astype(F32)
    for i in range(depth):
        m, j = i % N_MIXERS, i // N_MIXERS
        g1 = row(norm1_g[i])
        if m == 0:
            lambda_init = 0.8 - 0.6 * math.exp(-0.3 * i)
            gain = jnp.concatenate([jnp.tile(a_q_norm_g[j], 2 * A_HEADS) * (A_HEAD_DIM ** -0.5 * LOG2E),
                                    jnp.tile(a_k_norm_g[j], 2 * A_HEADS),
                                    jnp.ones((A_HEADS * A_V_DIM,), F32)]).reshape(1, -1)
            qkv = _proj(h, g1, a_w_in[j].astype(BF16), gain, n_norm_cols=2 * D_MODEL, seg=A_HEAD_DIM, aux=False)
            o = _attn_a(qkv, row(a_lambda_q1[j]), row(a_lambda_k1[j]), row(a_lambda_q2[j]), row(a_lambda_k2[j]),
                        a_subln_g[j].reshape(-1, 1).astype(F32), batch=batch, seq=seq, lambda_init=lambda_init)
            h = _outproj(o, a_w_out[j].astype(BF16), h)
        elif m == 1:
            h = _pool(h, g1, b_w_group[j].astype(BF16), row(b_scale[j]), batch=batch, seq=seq)
        else:
            gain = jnp.concatenate([jnp.tile(c_q_norm_g[j], C_HEADS) * (C_HEAD_DIM ** -0.5 * LOG2E),
                                    jnp.tile(c_k_norm_g[j], C_KV_HEADS),
                                    jnp.ones((C_COLS - C_V0,), F32)]).reshape(1, -1)
            proj, aux = _proj(h, g1, _pad_c_weight(c_w_in[j]), gain, n_norm_cols=C_V0, seg=C_HEAD_DIM, aux=True)
            iw_t = aux[:, :IDX_HEADS].reshape(batch, seq, IDX_HEADS).transpose(0, 2, 1)
            o = _dsa(proj, iw_t, batch=batch, seq=seq, topk=min(TOPK_MAX, seq // 4))
            h = _outproj(o, c_w_out[j].astype(BF16), h)
        h = _mlp(h, row(norm2_g[i]), mlp_w1[i], mlp_w2[i])
    return h.reshape(batch, seq, d)
```

```python
import functools
import math

import jax
import jax.numpy as jnp
from jax import lax
from jax.experimental import pallas as pl
from jax.experimental.pallas import tpu as pltpu

F32 = jnp.float32
BF16 = jnp.bfloat16

D_MODEL = 1024
N_MIXERS = 3
CHUNK = 64
CHUNK_SHIFT = 6
EPS = 1e-6
LANES = 128
MXU_DIM = 256

A_HEADS = 8
A_HEAD_DIM = 64
A_V_DIM = 128
POOL_WINDOWS = (2, 4, 8, 16)
POOL_GROUP = 256
POOL_HALO = 16
C_HEADS = 8
C_HEAD_DIM = 128
C_KV_HEADS = 2
IDX_HEADS = 8
IDX_DIM = 64
TOPK_MAX = 256

C_Q0, C_K0, C_V0, C_IQ0, C_IK0, C_IW0, C_COLS = 0, 1024, 1280, 1536, 2048, 2176, 2304

NEG = -1e30
LOG2E = math.log2(math.e)
V_ROWS = 128 + 16
VMEM_LIMIT = 56 * 1024 * 1024


def _cparams(sem):
    return pltpu.CompilerParams(dimension_semantics=sem, vmem_limit_bytes=VMEM_LIMIT)


def _rms_rows(x, g):
    ms = jnp.mean(x * x, axis=-1, keepdims=True)
    return (x * lax.rsqrt(ms + EPS)) * g


def _dot_nt(a, b):
    return lax.dot_general(a, b, (((1,), (1,)), ((), ())), preferred_element_type=F32)


def _proj_kernel(h_ref, g_ref, w_ref, gain_ref, seg_ref, o_ref, *aux_ref, n_norm_cols, seg):
    u = _rms_rows(h_ref[...], g_ref[...]).astype(BF16)
    n = w_ref.shape[1]
    for c0 in range(0, n, 2 * MXU_DIM):
        cw = min(2 * MXU_DIM, n - c0)
        acc = jnp.dot(u, w_ref[:, c0:c0 + cw], preferred_element_type=F32)
        for s0 in range(0, cw, MXU_DIM):
            cols = slice(c0 + s0, c0 + s0 + MXU_DIM)
            a = acc[:, s0:s0 + MXU_DIM]
            if c0 + s0 < n_norm_cols:
                sq = a * a
                hi = sq.astype(BF16)
                lo = (sq - hi.astype(F32)).astype(BF16)
                ssum = (jnp.dot(hi, seg_ref[...], preferred_element_type=F32)
                        + jnp.dot(lo, seg_ref[...], preferred_element_type=F32))
                a = (a * lax.rsqrt(ssum * (1.0 / seg) + EPS)) * gain_ref[:, cols]
            o_ref[:, cols] = a.astype(BF16)
        if aux_ref and c0 + cw == n:
            aux_ref[0][...] = acc[:, cw - LANES:]


def _proj(h, g, w, gain, *, n_norm_cols, seg, aux):
    t, d = h.shape
    n = w.shape[1]
    tm = min(512, t)
    assert n % MXU_DIM == 0 and n_norm_cols % MXU_DIM == 0 and t % tm == 0
    r = lax.broadcasted_iota(jnp.int32, (MXU_DIM, MXU_DIM), 0) // seg
    c = lax.broadcasted_iota(jnp.int32, (MXU_DIM, MXU_DIM), 1) // seg
    segm = (r == c).astype(BF16)
    out_shape = [jax.ShapeDtypeStruct((t, n), BF16)]
    out_specs = [pl.BlockSpec((tm, n), lambda i: (i, 0))]
    if aux:
        out_shape.append(jax.ShapeDtypeStruct((t, LANES), F32))
        out_specs.append(pl.BlockSpec((tm, LANES), lambda i: (i, 0)))
    res = pl.pallas_call(
        functools.partial(_proj_kernel, n_norm_cols=n_norm_cols, seg=seg),
        grid=(t // tm,),
        in_specs=[
            pl.BlockSpec((tm, d), lambda i: (i, 0)),
            pl.BlockSpec((1, d), lambda i: (0, 0)),
            pl.BlockSpec((d, n), lambda i: (0, 0)),
            pl.BlockSpec((1, n), lambda i: (0, 0)),
            pl.BlockSpec((MXU_DIM, MXU_DIM), lambda i: (0, 0)),
        ],
        out_specs=out_specs,
        out_shape=out_shape,
        compiler_params=_cparams(("parallel",)),
        name="proj",
    )(h, g, w, gain, segm)
    return res if aux else res[0]


def _outproj_kernel(o_ref, w_ref, h_ref, out_ref):
    out_ref[...] = h_ref[...] + jnp.dot(o_ref[...], w_ref[...], preferred_element_type=F32)


def _outproj(o, w, h):
    t, d = h.shape
    tm = min(1024, t)
    return pl.pallas_call(
        _outproj_kernel,
        grid=(t // tm,),
        in_specs=[
            pl.BlockSpec((tm, o.shape[1]), lambda i: (i, 0)),
            pl.BlockSpec(w.shape, lambda i: (0, 0)),
            pl.BlockSpec((tm, d), lambda i: (i, 0)),
        ],
        out_specs=pl.BlockSpec((tm, d), lambda i: (i, 0)),
        out_shape=jax.ShapeDtypeStruct((t, d), F32),
        compiler_params=_cparams(("parallel",)),
        name="outproj",
    )(o, w, h)


def _mlp_kernel(h_ref, g_ref, w1_ref, w2_ref, o_ref, u_ref, acc_ref):
    j = pl.program_id(1)

    @pl.when(j == 0)
    def _():
        u_ref[...] = _rms_rows(h_ref[...], g_ref[...]).astype(BF16)
        acc_ref[...] = jnp.zeros_like(acc_ref)

    hid = jnp.dot(u_ref[...], w1_ref[...].astype(BF16), preferred_element_type=F32)
    hid = jnp.square(jnp.maximum(hid, 0.0)).astype(BF16)
    acc_ref[...] += jnp.dot(hid, w2_ref[...].astype(BF16), preferred_element_type=F32)

    @pl.when(j == pl.num_programs(1) - 1)
    def _():
        o_ref[...] = h_ref[...] + acc_ref[...]


def _mlp(h, g, w1, w2, layer):
    t, d = h.shape
    f = w1.shape[2]
    tm = min(1024, t)
    tf = 1024
    return pl.pallas_call(
        _mlp_kernel,
        grid=(t // tm, f // tf),
        in_specs=[
            pl.BlockSpec((tm, d), lambda i, j: (i, 0)),
            pl.BlockSpec((1, d), lambda i, j: (0, 0)),
            pl.BlockSpec((None, d, tf), lambda i, j: (layer, 0, j)),
            pl.BlockSpec((None, tf, d), lambda i, j: (layer, j, 0)),
        ],
        out_specs=pl.BlockSpec((tm, d), lambda i, j: (i, 0)),
        out_shape=jax.ShapeDtypeStruct((t, d), F32),
        scratch_shapes=[pltpu.VMEM((tm, d), BF16), pltpu.VMEM((tm, d), F32)],
        compiler_params=_cparams(("parallel", "arbitrary")),
        name="mlp",
    )(h, g, w1, w2)


def _softmax_step_t(st, vt1, m_ref, acc_ref):
    m_prev = m_ref[...]
    m_next = jnp.maximum(m_prev, jnp.max(st, axis=0, keepdims=True))
    p = jnp.exp2(st - m_next).astype(BF16)
    alpha = jnp.exp2(m_prev - m_next)
    m_ref[...] = m_next
    acc_ref[...] = alpha * acc_ref[...] + jnp.dot(vt1, p, preferred_element_type=F32)


def _pipelined_tiles(n, scores, consume, buf_a, buf_b):
    n_pairs = (n - 1) // 2
    scores(0, buf_a)

    def pair(t, c):
        scores(2 * t + 1, buf_b)
        consume(2 * t, buf_a, False)
        scores(2 * t + 2, buf_a)
        consume(2 * t + 1, buf_b, False)
        return c

    lax.fori_loop(0, n_pairs, pair, 0)

    @pl.when(n - 2 * n_pairs == 1)
    def _():
        consume(n - 1, buf_a, True)

    @pl.when(n - 2 * n_pairs == 2)
    def _():
        scores(n - 1, buf_b)
        consume(n - 2, buf_a, False)
        consume(n - 1, buf_b, True)


def _store_vt(vt_ref, v_ref, cols, seq, blk):
    for c in range(seq // blk):
        rows = slice(c * blk, (c + 1) * blk)
        vt_ref[0:LANES, rows] = v_ref[rows, cols].astype(F32).T.astype(BF16)
    vt_ref[LANES:, :] = jnp.ones((V_ROWS - LANES, seq), BF16)


A_HEADS_PER_STEP = 2


def _attn_a_kernel(q_ref, k_ref, v_ref, lq1_ref, lk1_ref, lq2_ref, lk2_ref, subg_ref, o_ref,
                   vt_ref, m_ref, acc_ref, s0_ref, s1_ref, *, tq, tk, seq, lambda_init):
    i = pl.program_id(2)
    heads = range(A_HEADS_PER_STEP)
    hcols = lambda hh: slice(hh * LANES, (hh + 1) * LANES)

    @pl.when(i == 0)
    def _():
        for hh in heads:
            _store_vt(vt_ref.at[hh], v_ref, hcols(hh), seq, tk)

    lane = lax.broadcasted_iota(jnp.int32, (tq, LANES), 1)
    qm = []
    for hh in heads:
        q = q_ref[:, hcols(hh)].astype(F32)
        qm.append(jnp.concatenate([jnp.where(lane < A_HEAD_DIM, q, 0.0),
                                   jnp.where(lane >= A_HEAD_DIM, q, 0.0)], axis=0).astype(BF16))
    m_ref[...] = jnp.full(m_ref.shape, NEG, F32)
    acc_ref[...] = jnp.zeros_like(acc_ref)

    def scores(j, buf):
        ks = pl.multiple_of(j * tk, tk)
        for hh in heads:
            buf[hh] = _dot_nt(k_ref[pl.ds(ks, tk), hcols(hh)], qm[hh])

    def consume(j, buf, masked):
        ks = pl.multiple_of(j * tk, tk)
        if masked:
            kpos = ks + lax.broadcasted_iota(jnp.int32, (tk, 2 * tq), 0)
            qpos = i * tq + (lax.broadcasted_iota(jnp.int32, (tk, 2 * tq), 1) & (tq - 1))
            ok = (kpos >> CHUNK_SHIFT) <= (qpos >> CHUNK_SHIFT)
        for hh in heads:
            st = buf[hh]
            if masked:
                st = jnp.where(ok, st, NEG)
            _softmax_step_t(st, vt_ref[hh, :, pl.ds(ks, tk)], m_ref.at[hh], acc_ref.at[hh])

    _pipelined_tiles(((i + 1) * tq + tk - 1) // tk, scores, consume, s0_ref, s1_ref)

    lam = (jnp.exp(jnp.sum(lq1_ref[...] * lk1_ref[...], axis=1, keepdims=True))
           - jnp.exp(jnp.sum(lq2_ref[...] * lk2_ref[...], axis=1, keepdims=True)) + lambda_init)
    for hh in heads:
        acc = acc_ref[hh]
        ot = acc[0:LANES] / acc[LANES:LANES + 1]
        od = ot[:, :tq] - lam * ot[:, tq:]
        ms = jnp.mean(od * od, axis=0, keepdims=True)
        y = (od * lax.rsqrt(ms + EPS)) * subg_ref[...] * (1.0 - lambda_init)
        o_ref[:, hcols(hh)] = y.T.astype(BF16)


def _attn_a(qkv, lq1, lk1, lq2, lk2, subg, *, batch, seq, lambda_init):
    t = batch * seq
    tq = min(512, seq)
    tk = min(512, seq)
    nq = seq // tq
    hp = A_HEADS_PER_STEP
    ngrp = A_HEADS // hp
    vec = lambda n: pl.BlockSpec((1, n), lambda b, h, i: (0, 0))
    return pl.pallas_call(
        functools.partial(_attn_a_kernel, tq=tq, tk=tk, seq=seq, lambda_init=lambda_init),
        grid=(batch, ngrp, nq),
        in_specs=[
            pl.BlockSpec((tq, hp * LANES), lambda b, h, i: (b * nq + i, h)),
            pl.BlockSpec((seq, hp * LANES), lambda b, h, i: (b, ngrp + h)),
            pl.BlockSpec((seq, hp * LANES), lambda b, h, i: (b, 2 * ngrp + h)),
            vec(A_HEAD_DIM), vec(A_HEAD_DIM), vec(A_HEAD_DIM), vec(A_HEAD_DIM),
            pl.BlockSpec((A_V_DIM, 1), lambda b, h, i: (0, 0)),
        ],
        out_specs=pl.BlockSpec((tq, hp * LANES), lambda b, h, i: (b * nq + i, h)),
        out_shape=jax.ShapeDtypeStruct((t, A_HEADS * A_V_DIM), BF16),
        scratch_shapes=[pltpu.VMEM((hp, V_ROWS, seq), BF16), pltpu.VMEM((hp, 1, 2 * tq), F32),
                        pltpu.VMEM((hp, V_ROWS, 2 * tq), F32)] + [pltpu.VMEM((hp, tk, 2 * tq), F32)] * 2,
        compiler_params=_cparams(("arbitrary", "arbitrary", "arbitrary")),
        name="attn_a",
    )(qkv, qkv, qkv, lq1, lk1, lq2, lk2, subg)


def _pool_kernel(h_ref, halo_ref, g_ref, wg_ref, sc_ref, o_ref, ext_ref, *, ts):
    i = pl.program_id(1)
    x = h_ref[...]
    u = _rms_rows(x, g_ref[...])
    uh = _rms_rows(halo_ref[...], g_ref[...])
    ext_ref[0:POOL_HALO, :] = jnp.where(i > 0, uh, 0.0)
    ext_ref[POOL_HALO:, :] = u
    pos = i * ts + lax.broadcasted_iota(jnp.int32, (ts, 1), 0)
    for g, w in enumerate(POOL_WINDOWS):
        cols = slice(g * POOL_GROUP, (g + 1) * POOL_GROUP)
        win = ext_ref[POOL_HALO:, cols]
        for d in range(1, w):
            win = win + ext_ref[pl.ds(POOL_HALO - d, ts), cols]
        count = jnp.minimum(pos + 1, w).astype(F32)
        pooled = win / count - u[:, cols]
        y = jnp.dot(pooled.astype(BF16), wg_ref[g], preferred_element_type=F32)
        o_ref[:, cols] = x[:, cols] + y * sc_ref[:, cols]


def _pool(h, g, wg, scale, *, batch, seq):
    t, d = h.shape
    ts = min(512, seq)
    ns = seq // ts
    hb = ts // POOL_HALO
    return pl.pallas_call(
        functools.partial(_pool_kernel, ts=ts),
        grid=(batch, ns),
        in_specs=[
            pl.BlockSpec((ts, d), lambda b, i: (b * ns + i, 0)),
            pl.BlockSpec((POOL_HALO, d), lambda b, i: (jnp.maximum((b * ns + i) * hb - 1, 0), 0)),
            pl.BlockSpec((1, d), lambda b, i: (0, 0)),
            pl.BlockSpec(wg.shape, lambda b, i: (0, 0, 0)),
            pl.BlockSpec((1, d), lambda b, i: (0, 0)),
        ],
        out_specs=pl.BlockSpec((ts, d), lambda b, i: (b * ns + i, 0)),
        out_shape=jax.ShapeDtypeStruct((t, d), F32),
        scratch_shapes=[pltpu.VMEM((ts + POOL_HALO, d), F32)],
        compiler_params=_cparams(("parallel", "arbitrary")),
        name="pool",
    )(h, h, g, wg, scale)


def _masked_heads(x, n_pairs, rows):
    xf = x.astype(F32)
    lane = lax.broadcasted_iota(jnp.int32, (rows, LANES), 1)
    out = []
    for p in range(n_pairs):
        xp = xf[:, p * LANES:(p + 1) * LANES]
        out.append(jnp.where(lane < IDX_DIM, xp, 0.0).astype(BF16))
        out.append(jnp.where(lane >= IDX_DIM, xp, 0.0).astype(BF16))
    return out


def _sum_sublane_groups(m, rows):
    parts = [m[r:r + 8] for r in range(0, rows, 8)]
    while len(parts) > 1:
        parts = [parts[a] + parts[a + 1] for a in range(0, len(parts), 2)]
    return parts[0]


def _dsa_kernel(q_ref, k_ref, v_ref, iq_ref, ik_ref, iw_ref, o_ref, sc_ref, scb_ref, vt_ref, m_ref, acc_ref,
                s0_ref, s1_ref, *, tq, kb, seq, topk, idx_bits, idx_scale):
    i = pl.program_id(1)
    rep = C_HEADS // C_KV_HEADS

    @pl.when(i == 0)
    def _():
        for g in range(C_KV_HEADS):
            _store_vt(vt_ref.at[g], v_ref, slice(g * C_HEAD_DIM, (g + 1) * C_HEAD_DIM), seq, kb)

    nblk = ((i + 1) * tq) // kb
    iw = iw_ref[0] * idx_scale
    iqh = _masked_heads(iq_ref[...], IDX_HEADS // 2, tq)
    qchunk = (i * tq + lax.broadcasted_iota(jnp.int32, (kb, tq), 1)) >> CHUNK_SHIFT
    krow = lax.broadcasted_iota(jnp.int32, (kb, tq), 0)

    def score_block(r):
        ks = pl.multiple_of(r * kb, kb)
        ikb = ik_ref[pl.ds(ks, kb), :]
        acc = jnp.zeros((kb, tq), F32)
        for h in range(IDX_HEADS):
            acc = acc + iw[h:h + 1, :] * jnp.maximum(_dot_nt(ikb, iqh[h]), 0.0)
        acc = jnp.where(((ks + krow) >> CHUNK_SHIFT) <= qchunk, acc, -jnp.inf)
        sc_ref[pl.ds(ks, kb), :] = acc
        scb_ref[pl.ds(ks, kb), :] = acc.astype(BF16)

    def score_pair(t, c):
        score_block(2 * t)
        score_block(2 * t + 1)
        return c

    lax.fori_loop(0, nblk // 2, score_pair, 0)

    @pl.when(nblk % 2 == 1)
    def _():
        score_block(nblk - 1)

    n_acc = 4

    def count(pred):
        def body(r, accs):
            ks = pl.multiple_of(r * kb, kb)
            p = pred(sc_ref[pl.ds(ks, kb), :], ks)
            accs = list(accs)
            for g8 in range(kb // 8):
                a = accs[g8 % n_acc]
                accs[g8 % n_acc] = jnp.where(p[g8 * 8:(g8 + 1) * 8], a + 1.0, a)
            return tuple(accs)
        accs = lax.fori_loop(0, nblk, body, (jnp.zeros((8, tq), F32),) * n_acc)
        return jnp.sum(sum(accs), axis=0, keepdims=True)

    int_min = jnp.int32(-2 ** 31)

    def key_to_f32(key_u):
        ks_ = key_u ^ int_min
        bits = jnp.where(ks_ >= 0, ks_, ks_ ^ jnp.int32(0x7FFFFFFF))
        return lax.bitcast_convert_type(bits, F32)

    one_bf = jnp.ones((), BF16)

    def count_coarse(cand_bf):
        def body(r, accs):
            blk = scb_ref[pl.ds(pl.multiple_of(r * kb, kb), kb), :]
            accs = list(accs)
            for g16 in range(kb // 16):
                a = accs[g16 % n_acc]
                accs[g16 % n_acc] = jnp.where(blk[g16 * 16:(g16 + 1) * 16] >= cand_bf, a + one_bf, a)
            return tuple(accs)
        accs = lax.fori_loop(0, nblk, body, (jnp.zeros((16, tq), BF16),) * n_acc)
        return jnp.sum(sum(a.astype(F32) for a in accs), axis=0, keepdims=True)

    def coarse_body(b, t_u):
        cand_u = t_u | lax.shift_left(jnp.int32(1), 31 - b)
        cnt = count_coarse(key_to_f32(cand_u).astype(BF16))
        return jnp.where(cnt >= topk, cand_u, t_u)

    p_u = lax.fori_loop(0, 16, coarse_body, jnp.zeros((1, tq), jnp.int32))

    lo_u = p_u - jnp.int32(0x8000)

    def fine_body(b, off):
        cand_off = off | lax.shift_left(jnp.int32(1), 16 - b)
        cand = key_to_f32(lo_u + cand_off)
        cnt = count(lambda blk, ks: blk >= cand)
        return jnp.where(cnt >= topk, cand_off, off)

    t_u = lo_u + lax.fori_loop(0, 17, fine_body, jnp.zeros((1, tq), jnp.int32))
    n_adm = (((i * tq + lax.broadcasted_iota(jnp.int32, (1, tq), 1)) >> CHUNK_SHIFT) + 1) << CHUNK_SHIFT
    thr = jnp.where(n_adm < topk, -jnp.inf, key_to_f32(t_u))
    n_ge = count(lambda blk, ks: blk >= thr)

    def tie_cut():
        need = topk - count(lambda blk, ks: blk > thr)

        def idx_body(b, j_u):
            cand = j_u | lax.shift_left(jnp.int32(1), idx_bits - 1 - b)
            cnt = count(lambda blk, ks: (blk == thr) & ((ks + krow) < cand))
            return jnp.where(cnt < need, cand, j_u)

        return lax.fori_loop(0, idx_bits, idx_body, jnp.zeros((1, tq), jnp.int32))

    def no_cut():
        return jnp.full((1, tq), 2 ** idx_bits - 1, jnp.int32)

    jcut = lax.cond(jnp.max(n_ge) > topk, tie_cut, no_cut)
    jcut = jnp.where(thr == -jnp.inf, -1, jcut)

    m_ref[...] = jnp.full(m_ref.shape, NEG, F32)
    acc_ref[...] = jnp.zeros_like(acc_ref)
    qg = [jnp.concatenate([q_ref[:, (g * rep + e) * C_HEAD_DIM:(g * rep + e + 1) * C_HEAD_DIM]
                           for e in range(rep)], axis=0) for g in range(C_KV_HEADS)]

    def scores(r, buf):
        ks = pl.multiple_of(r * kb, kb)
        for g in range(C_KV_HEADS):
            buf[g] = _dot_nt(k_ref[pl.ds(ks, kb), g * C_HEAD_DIM:(g + 1) * C_HEAD_DIM], qg[g])

    def consume(r, buf, is_last):
        ks = pl.multiple_of(r * kb, kb)
        sc = sc_ref[pl.ds(ks, kb), :]
        sel = (sc > thr) | ((sc == thr) & ((ks + krow) <= jcut))
        bias = jnp.where(sel, 0.0, NEG)
        bias = jnp.concatenate([bias] * rep, axis=1)
        for g in range(C_KV_HEADS):
            _softmax_step_t(buf[g] + bias, vt_ref[g, :, pl.ds(ks, kb)], m_ref.at[g], acc_ref.at[g])

    _pipelined_tiles(nblk, scores, consume, s0_ref, s1_ref)
    for g in range(C_KV_HEADS):
        a = acc_ref[g]
        ot = a[0:LANES] / a[LANES:LANES + 1]
        for e in range(rep):
            hh = g * rep + e
            o_ref[:, hh * C_HEAD_DIM:(hh + 1) * C_HEAD_DIM] = ot[:, e * tq:(e + 1) * tq].T.astype(BF16)


def _dsa(proj, iw_t, *, batch, seq, topk):
    t = batch * seq
    tq = min(256, seq)
    nq = seq // tq
    kvw = C_KV_HEADS * C_HEAD_DIM
    iqw = IDX_HEADS * IDX_DIM
    return pl.pallas_call(
        functools.partial(_dsa_kernel, tq=tq, kb=tq, seq=seq, topk=float(topk), idx_bits=(seq - 1).bit_length(),
                          idx_scale=IDX_HEADS ** -0.5 * IDX_DIM ** -0.5),
        grid=(batch, nq),
        in_specs=[
            pl.BlockSpec((tq, C_HEADS * C_HEAD_DIM), lambda b, i: (b * nq + i, 0)),
            pl.BlockSpec((seq, kvw), lambda b, i: (b, C_K0 // kvw)),
            pl.BlockSpec((seq, kvw), lambda b, i: (b, C_V0 // kvw)),
            pl.BlockSpec((tq, iqw), lambda b, i: (b * nq + i, C_IQ0 // iqw)),
            pl.BlockSpec((seq, LANES), lambda b, i: (b, C_IK0 // LANES)),
            pl.BlockSpec((1, 8, tq), lambda b, i: (b, 0, i)),
        ],
        out_specs=pl.BlockSpec((tq, C_HEADS * C_HEAD_DIM), lambda b, i: (b * nq + i, 0)),
        out_shape=jax.ShapeDtypeStruct((t, C_HEADS * C_HEAD_DIM), BF16),
        scratch_shapes=[pltpu.VMEM((seq, tq), F32), pltpu.VMEM((seq, tq), BF16),
                        pltpu.VMEM((C_KV_HEADS, V_ROWS, seq), BF16),
                        pltpu.VMEM((C_KV_HEADS, 1, C_HEADS // C_KV_HEADS * tq), F32),
                        pltpu.VMEM((C_KV_HEADS, V_ROWS, C_HEADS // C_KV_HEADS * tq), F32)]
                       + [pltpu.VMEM((C_KV_HEADS, tq, C_HEADS // C_KV_HEADS * tq), F32)] * 2,
        compiler_params=_cparams(("arbitrary", "arbitrary")),
        name="dsa",
    )(proj, proj, proj, proj, proj, iw_t)


def _pad_c_weight(w):
    iw = w[:, 2112:2120]
    ik = w[:, 2048:2112]
    pad = jnp.zeros((w.shape[0], C_COLS - C_IW0 - IDX_HEADS), w.dtype)
    return jnp.concatenate([w[:, :2048], ik, ik, iw, pad], axis=1).astype(BF16)


def kernel(x, norm1_g, norm2_g, a_w_in, a_q_norm_g, a_k_norm_g, a_lambda_q1, a_lambda_k1, a_lambda_q2,
           a_lambda_k2, a_subln_g, a_w_out, b_w_group, b_scale, c_w_in, c_q_norm_g, c_k_norm_g, c_w_out,
           mlp_w1, mlp_w2):
    batch, seq, d = x.shape
    depth = norm1_g.shape[0]
    h = x.reshape(batch * seq, d)
    row = lambda v: v.reshape(1, -1).astype(F32)
    for i in range(depth):
        m, j = i % N_MIXERS, i // N_MIXERS
        g1 = row(norm1_g[i])
        if m == 0:
            lambda_init = 0.8 - 0.6 * math.exp(-0.3 * i)
            gain = jnp.concatenate([jnp.tile(a_q_norm_g[j], 2 * A_HEADS) * (A_HEAD_DIM ** -0.5 * LOG2E),
                                    jnp.tile(a_k_norm_g[j], 2 * A_HEADS),
                                    jnp.ones((A_HEADS * A_V_DIM,), F32)]).reshape(1, -1)
            qkv = _proj(h, g1, a_w_in[j].astype(BF16), gain, n_norm_cols=2 * D_MODEL, seg=A_HEAD_DIM, aux=False)
            o = _attn_a(qkv, row(a_lambda_q1[j]), row(a_lambda_k1[j]), row(a_lambda_q2[j]), row(a_lambda_k2[j]),
                        a_subln_g[j].reshape(-1, 1).astype(F32), batch=batch, seq=seq, lambda_init=lambda_init)
            h = _outproj(o, a_w_out[j].astype(BF16), h)
        elif m == 1:
            h = _pool(h, g1, b_w_group[j].astype(BF16), row(b_scale[j]), batch=batch, seq=seq)
        else:
            gain = jnp.concatenate([jnp.tile(c_q_norm_g[j], C_HEADS) * (C_HEAD_DIM ** -0.5 * LOG2E),
                                    jnp.tile(c_k_norm_g[j], C_KV_HEADS),
                                    jnp.ones((C_COLS - C_V0,), F32)]).reshape(1, -1)
            proj, aux = _proj(h, g1, _pad_c_weight(c_w_in[j]), gain, n_norm_cols=C_V0, seg=C_HEAD_DIM, aux=True)
            iw_t = aux[:, :IDX_HEADS].reshape(batch, seq, IDX_HEADS).transpose(0, 2, 1)
            o = _dsa(proj, iw_t, batch=batch, seq=seq, topk=min(TOPK_MAX, seq // 4))
            h = _outproj(o, c_w_out[j].astype(BF16), h)
        h = _mlp(h, row(norm2_g[i]), mlp_w1, mlp_w2, i)
    return h.reshape(batch, seq, d)
```

```python
import functools
import math

import jax
import jax.numpy as jnp
from jax import lax
from jax.experimental import pallas as pl
from jax.experimental.pallas import tpu as pltpu

F32 = jnp.float32
BF16 = jnp.bfloat16

D_MODEL = 1024
N_MIXERS = 3
CHUNK = 64
CHUNK_SHIFT = 6
EPS = 1e-6
LANES = 128
MXU_DIM = 256

A_HEADS = 8
A_HEAD_DIM = 64
A_V_DIM = 128
POOL_WINDOWS = (2, 4, 8, 16)
POOL_GROUP = 256
POOL_HALO = 16
C_HEADS = 8
C_HEAD_DIM = 128
C_KV_HEADS = 2
IDX_HEADS = 8
IDX_DIM = 64
TOPK_MAX = 256

C_Q0, C_K0, C_V0, C_IQ0, C_IK0, C_IW0, C_COLS = 0, 1024, 1280, 1536, 2048, 2176, 2304

NEG = -1e30
LOG2E = math.log2(math.e)
V_ROWS = 128 + 16
VMEM_LIMIT = 56 * 1024 * 1024


def _cparams(sem):
    return pltpu.CompilerParams(dimension_semantics=sem, vmem_limit_bytes=VMEM_LIMIT)


def _rms_rows(x, g):
    ms = jnp.mean(x * x, axis=-1, keepdims=True)
    return (x * lax.rsqrt(ms + EPS)) * g


def _dot_nt(a, b):
    return lax.dot_general(a, b, (((1,), (1,)), ((), ())), preferred_element_type=F32)


def _proj_kernel(h_ref, g_ref, w_ref, gain_ref, seg_ref, o_ref, *aux_ref, n_norm_cols, seg):
    u = _rms_rows(h_ref[...], g_ref[...]).astype(BF16)
    n = w_ref.shape[1]
    for c0 in range(0, n, 2 * MXU_DIM):
        cw = min(2 * MXU_DIM, n - c0)
        acc = jnp.dot(u, w_ref[:, c0:c0 + cw], preferred_element_type=F32)
        for s0 in range(0, cw, MXU_DIM):
            cols = slice(c0 + s0, c0 + s0 + MXU_DIM)
            a = acc[:, s0:s0 + MXU_DIM]
            if c0 + s0 < n_norm_cols:
                sq = a * a
                hi = sq.astype(BF16)
                lo = (sq - hi.astype(F32)).astype(BF16)
                ssum = (jnp.dot(hi, seg_ref[...], preferred_element_type=F32)
                        + jnp.dot(lo, seg_ref[...], preferred_element_type=F32))
                a = (a * lax.rsqrt(ssum * (1.0 / seg) + EPS)) * gain_ref[:, cols]
            o_ref[:, cols] = a.astype(BF16)
        if aux_ref and c0 + cw == n:
            aux_ref[0][...] = acc[:, cw - LANES:]


def _proj(h, g, w, gain, *, n_norm_cols, seg, aux):
    t, d = h.shape
    n = w.shape[1]
    tm = min(512, t)
    assert n % MXU_DIM == 0 and n_norm_cols % MXU_DIM == 0 and t % tm == 0
    r = lax.broadcasted_iota(jnp.int32, (MXU_DIM, MXU_DIM), 0) // seg
    c = lax.broadcasted_iota(jnp.int32, (MXU_DIM, MXU_DIM), 1) // seg
    segm = (r == c).astype(BF16)
    out_shape = [jax.ShapeDtypeStruct((t, n), BF16)]
    out_specs = [pl.BlockSpec((tm, n), lambda i: (i, 0))]
    if aux:
        out_shape.append(jax.ShapeDtypeStruct((t, LANES), F32))
        out_specs.append(pl.BlockSpec((tm, LANES), lambda i: (i, 0)))
    res = pl.pallas_call(
        functools.partial(_proj_kernel, n_norm_cols=n_norm_cols, seg=seg),
        grid=(t // tm,),
        in_specs=[
            pl.BlockSpec((tm, d), lambda i: (i, 0)),
            pl.BlockSpec((1, d), lambda i: (0, 0)),
            pl.BlockSpec((d, n), lambda i: (0, 0)),
            pl.BlockSpec((1, n), lambda i: (0, 0)),
            pl.BlockSpec((MXU_DIM, MXU_DIM), lambda i: (0, 0)),
        ],
        out_specs=out_specs,
        out_shape=out_shape,
        compiler_params=_cparams(("parallel",)),
        name="proj",
    )(h, g, w, gain, segm)
    return res if aux else res[0]


def _outproj_kernel(o_ref, w_ref, h_ref, out_ref):
    out_ref[...] = h_ref[...] + jnp.dot(o_ref[...], w_ref[...], preferred_element_type=F32)


def _outproj(o, w, h):
    t, d = h.shape
    tm = min(1024, t)
    return pl.pallas_call(
        _outproj_kernel,
        grid=(t // tm,),
        in_specs=[
            pl.BlockSpec((tm, o.shape[1]), lambda i: (i, 0)),
            pl.BlockSpec(w.shape, lambda i: (0, 0)),
            pl.BlockSpec((tm, d), lambda i: (i, 0)),
        ],
        out_specs=pl.BlockSpec((tm, d), lambda i: (i, 0)),
        out_shape=jax.ShapeDtypeStruct((t, d), F32),
        compiler_params=_cparams(("parallel",)),
        name="outproj",
    )(o, w, h)


def _mlp_kernel(h_ref, g_ref, w1_ref, w2_ref, o_ref, u_ref, acc_ref):
    j = pl.program_id(1)

    @pl.when(j == 0)
    def _():
        u_ref[...] = _rms_rows(h_ref[...], g_ref[...]).astype(BF16)
        acc_ref[...] = jnp.zeros_like(acc_ref)

    hid = jnp.dot(u_ref[...], w1_ref[...].astype(BF16), preferred_element_type=F32)
    hid = jnp.square(jnp.maximum(hid, 0.0)).astype(BF16)
    acc_ref[...] += jnp.dot(hid, w2_ref[...].astype(BF16), preferred_element_type=F32)

    @pl.when(j == pl.num_programs(1) - 1)
    def _():
        o_ref[...] = h_ref[...] + acc_ref[...]


def _mlp(h, g, w1, w2, layer):
    t, d = h.shape
    f = w1.shape[2]
    tm = min(1024, t)
    tf = 1024
    return pl.pallas_call(
        _mlp_kernel,
        grid=(t // tm, f // tf),
        in_specs=[
            pl.BlockSpec((tm, d), lambda i, j: (i, 0)),
            pl.BlockSpec((1, d), lambda i, j: (0, 0)),
            pl.BlockSpec((None, d, tf), lambda i, j: (layer, 0, j)),
            pl.BlockSpec((None, tf, d), lambda i, j: (layer, j, 0)),
        ],
        out_specs=pl.BlockSpec((tm, d), lambda i, j: (i, 0)),
        out_shape=jax.ShapeDtypeStruct((t, d), F32),
        scratch_shapes=[pltpu.VMEM((tm, d), BF16), pltpu.VMEM((tm, d), F32)],
        compiler_params=_cparams(("parallel", "arbitrary")),
        name="mlp",
    )(h, g, w1, w2)


def _softmax_step_t(st, vt1, m_ref, acc_ref):
    m_prev = m_ref[...]
    m_next = jnp.maximum(m_prev, jnp.max(st, axis=0, keepdims=True))
    p = jnp.exp2(st - m_next).astype(BF16)
    alpha = jnp.exp2(m_prev - m_next)
    m_ref[...] = m_next
    acc_ref[...] = alpha * acc_ref[...] + jnp.dot(vt1, p, preferred_element_type=F32)


def _pipelined_tiles(n, scores, consume, buf_a, buf_b):
    n_pairs = (n - 1) // 2
    scores(0, buf_a)

    def pair(t, c):
        scores(2 * t + 1, buf_b)
        consume(2 * t, buf_a, False)
        scores(2 * t + 2, buf_a)
        consume(2 * t + 1, buf_b, False)
        return c

    lax.fori_loop(0, n_pairs, pair, 0)

    @pl.when(n - 2 * n_pairs == 1)
    def _():
        consume(n - 1, buf_a, True)

    @pl.when(n - 2 * n_pairs == 2)
    def _():
        scores(n - 1, buf_b)
        consume(n - 2, buf_a, False)
        consume(n - 1, buf_b, True)


def _store_vt(vt_ref, v_ref, cols, seq, blk):
    for c in range(seq // blk):
        rows = slice(c * blk, (c + 1) * blk)
        vt_ref[0:LANES, rows] = v_ref[rows, cols].astype(F32).T.astype(BF16)
    vt_ref[LANES:, :] = jnp.ones((V_ROWS - LANES, seq), BF16)


A_HEADS_PER_STEP = 4


def _attn_a_kernel(q_ref, k_ref, v_ref, lq1_ref, lk1_ref, lq2_ref, lk2_ref, subg_ref, o_ref,
                   vt_ref, m_ref, acc_ref, s0_ref, s1_ref, *, tq, tk, seq, lambda_init):
    i = pl.program_id(2)
    heads = range(A_HEADS_PER_STEP)
    hcols = lambda hh: slice(hh * LANES, (hh + 1) * LANES)

    @pl.when(i == 0)
    def _():
        for hh in heads:
            _store_vt(vt_ref.at[hh], v_ref, hcols(hh), seq, tk)

    lane = lax.broadcasted_iota(jnp.int32, (tq, LANES), 1)
    qm = []
    for hh in heads:
        q = q_ref[:, hcols(hh)].astype(F32)
        qm.append(jnp.concatenate([jnp.where(lane < A_HEAD_DIM, q, 0.0),
                                   jnp.where(lane >= A_HEAD_DIM, q, 0.0)], axis=0).astype(BF16))
    m_ref[...] = jnp.full(m_ref.shape, NEG, F32)
    acc_ref[...] = jnp.zeros_like(acc_ref)

    def scores(j, buf):
        ks = pl.multiple_of(j * tk, tk)
        for hh in heads:
            buf[hh] = _dot_nt(k_ref[pl.ds(ks, tk), hcols(hh)], qm[hh])

    def consume(j, buf, masked):
        ks = pl.multiple_of(j * tk, tk)
        if masked:
            kpos = ks + lax.broadcasted_iota(jnp.int32, (tk, 2 * tq), 0)
            qpos = i * tq + (lax.broadcasted_iota(jnp.int32, (tk, 2 * tq), 1) & (tq - 1))
            ok = (kpos >> CHUNK_SHIFT) <= (qpos >> CHUNK_SHIFT)
        for hh in heads:
            st = buf[hh]
            if masked:
                st = jnp.where(ok, st, NEG)
            _softmax_step_t(st, vt_ref[hh, :, pl.ds(ks, tk)], m_ref.at[hh], acc_ref.at[hh])

    _pipelined_tiles(((i + 1) * tq + tk - 1) // tk, scores, consume, s0_ref, s1_ref)

    lam = (jnp.exp(jnp.sum(lq1_ref[...] * lk1_ref[...], axis=1, keepdims=True))
           - jnp.exp(jnp.sum(lq2_ref[...] * lk2_ref[...], axis=1, keepdims=True)) + lambda_init)
    for hh in heads:
        acc = acc_ref[hh]
        ot = acc[0:LANES] / acc[LANES:LANES + 1]
        od = ot[:, :tq] - lam * ot[:, tq:]
        ms = jnp.mean(od * od, axis=0, keepdims=True)
        y = (od * lax.rsqrt(ms + EPS)) * subg_ref[...] * (1.0 - lambda_init)
        o_ref[:, hcols(hh)] = y.T.astype(BF16)


def _attn_a(qkv, lq1, lk1, lq2, lk2, subg, *, batch, seq, lambda_init):
    t = batch * seq
    tq = min(512, seq)
    tk = min(512, seq)
    nq = seq // tq
    hp = A_HEADS_PER_STEP
    ngrp = A_HEADS // hp
    vec = lambda n: pl.BlockSpec((1, n), lambda b, h, i: (0, 0))
    return pl.pallas_call(
        functools.partial(_attn_a_kernel, tq=tq, tk=tk, seq=seq, lambda_init=lambda_init),
        grid=(batch, ngrp, nq),
        in_specs=[
            pl.BlockSpec((tq, hp * LANES), lambda b, h, i: (b * nq + i, h)),
            pl.BlockSpec((seq, hp * LANES), lambda b, h, i: (b, ngrp + h)),
            pl.BlockSpec((seq, hp * LANES), lambda b, h, i: (b, 2 * ngrp + h)),
            vec(A_HEAD_DIM), vec(A_HEAD_DIM), vec(A_HEAD_DIM), vec(A_HEAD_DIM),
            pl.BlockSpec((A_V_DIM, 1), lambda b, h, i: (0, 0)),
        ],
        out_specs=pl.BlockSpec((tq, hp * LANES), lambda b, h, i: (b * nq + i, h)),
        out_shape=jax.ShapeDtypeStruct((t, A_HEADS * A_V_DIM), BF16),
        scratch_shapes=[pltpu.VMEM((hp, V_ROWS, seq), BF16), pltpu.VMEM((hp, 1, 2 * tq), F32),
                        pltpu.VMEM((hp, V_ROWS, 2 * tq), F32)] + [pltpu.VMEM((hp, tk, 2 * tq), F32)] * 2,
        compiler_params=_cparams(("arbitrary", "arbitrary", "arbitrary")),
        name="attn_a",
    )(qkv, qkv, qkv, lq1, lk1, lq2, lk2, subg)


def _pool_kernel(h_ref, halo_ref, g_ref, wg_ref, sc_ref, o_ref, ext_ref, *, ts):
    i = pl.program_id(1)
    x = h_ref[...]
    u = _rms_rows(x, g_ref[...])
    uh = _rms_rows(halo_ref[...], g_ref[...])
    ext_ref[0:POOL_HALO, :] = jnp.where(i > 0, uh, 0.0)
    ext_ref[POOL_HALO:, :] = u
    pos = i * ts + lax.broadcasted_iota(jnp.int32, (ts, 1), 0)
    for g, w in enumerate(POOL_WINDOWS):
        cols = slice(g * POOL_GROUP, (g + 1) * POOL_GROUP)
        win = ext_ref[POOL_HALO:, cols]
        for d in range(1, w):
            win = win + ext_ref[pl.ds(POOL_HALO - d, ts), cols]
        count = jnp.minimum(pos + 1, w).astype(F32)
        pooled = win / count - u[:, cols]
        y = jnp.dot(pooled.astype(BF16), wg_ref[g], preferred_element_type=F32)
        o_ref[:, cols] = x[:, cols] + y * sc_ref[:, cols]


def _pool(h, g, wg, scale, *, batch, seq):
    t, d = h.shape
    ts = min(512, seq)
    ns = seq // ts
    hb = ts // POOL_HALO
    return pl.pallas_call(
        functools.partial(_pool_kernel, ts=ts),
        grid=(batch, ns),
        in_specs=[
            pl.BlockSpec((ts, d), lambda b, i: (b * ns + i, 0)),
            pl.BlockSpec((POOL_HALO, d), lambda b, i: (jnp.maximum((b * ns + i) * hb - 1, 0), 0)),
            pl.BlockSpec((1, d), lambda b, i: (0, 0)),
            pl.BlockSpec(wg.shape, lambda b, i: (0, 0, 0)),
            pl.BlockSpec((1, d), lambda b, i: (0, 0)),
        ],
        out_specs=pl.BlockSpec((ts, d), lambda b, i: (b * ns + i, 0)),
        out_shape=jax.ShapeDtypeStruct((t, d), F32),
        scratch_shapes=[pltpu.VMEM((ts + POOL_HALO, d), F32)],
        compiler_params=_cparams(("parallel", "arbitrary")),
        name="pool",
    )(h, h, g, wg, scale)


def _masked_heads(x, n_pairs, rows):
    xf = x.astype(F32)
    lane = lax.broadcasted_iota(jnp.int32, (rows, LANES), 1)
    out = []
    for p in range(n_pairs):
        xp = xf[:, p * LANES:(p + 1) * LANES]
        out.append(jnp.where(lane < IDX_DIM, xp, 0.0).astype(BF16))
        out.append(jnp.where(lane >= IDX_DIM, xp, 0.0).astype(BF16))
    return out


def _sum_sublane_groups(m, rows):
    parts = [m[r:r + 8] for r in range(0, rows, 8)]
    while len(parts) > 1:
        parts = [parts[a] + parts[a + 1] for a in range(0, len(parts), 2)]
    return parts[0]


def _dsa_kernel(q_ref, k_ref, v_ref, iq_ref, ik_ref, iw_ref, o_ref, sc_ref, scb_ref, vt_ref, m_ref, acc_ref,
                s0_ref, s1_ref, *, tq, kb, seq, topk, idx_bits, idx_scale):
    i = pl.program_id(1)
    rep = C_HEADS // C_KV_HEADS

    @pl.when(i == 0)
    def _():
        for g in range(C_KV_HEADS):
            _store_vt(vt_ref.at[g], v_ref, slice(g * C_HEAD_DIM, (g + 1) * C_HEAD_DIM), seq, kb)

    nblk = ((i + 1) * tq) // kb
    iw = iw_ref[0] * idx_scale
    iqh = _masked_heads(iq_ref[...], IDX_HEADS // 2, tq)
    qchunk = (i * tq + lax.broadcasted_iota(jnp.int32, (kb, tq), 1)) >> CHUNK_SHIFT
    krow = lax.broadcasted_iota(jnp.int32, (kb, tq), 0)

    def score_block(r):
        ks = pl.multiple_of(r * kb, kb)
        ikb = ik_ref[pl.ds(ks, kb), :]
        acc = jnp.zeros((kb, tq), F32)
        for h in range(IDX_HEADS):
            acc = acc + iw[h:h + 1, :] * jnp.maximum(_dot_nt(ikb, iqh[h]), 0.0)
        acc = jnp.where(((ks + krow) >> CHUNK_SHIFT) <= qchunk, acc, -jnp.inf)
        sc_ref[pl.ds(ks, kb), :] = acc
        scb_ref[pl.ds(ks, kb), :] = acc.astype(BF16)

    def score_pair(t, c):
        score_block(2 * t)
        score_block(2 * t + 1)
        return c

    lax.fori_loop(0, nblk // 2, score_pair, 0)

    @pl.when(nblk % 2 == 1)
    def _():
        score_block(nblk - 1)

    n_acc = 4

    def count(pred):
        def body(r, accs):
            ks = pl.multiple_of(r * kb, kb)
            p = pred(sc_ref[pl.ds(ks, kb), :], ks)
            accs = list(accs)
            for g8 in range(kb // 8):
                a = accs[g8 % n_acc]
                accs[g8 % n_acc] = jnp.where(p[g8 * 8:(g8 + 1) * 8], a + 1.0, a)
            return tuple(accs)
        accs = lax.fori_loop(0, nblk, body, (jnp.zeros((8, tq), F32),) * n_acc)
        return jnp.sum(sum(accs), axis=0, keepdims=True)

    int_min = jnp.int32(-2 ** 31)

    def key_to_f32(key_u):
        ks_ = key_u ^ int_min
        bits = jnp.where(ks_ >= 0, ks_, ks_ ^ jnp.int32(0x7FFFFFFF))
        return lax.bitcast_convert_type(bits, F32)

    one_bf = jnp.ones((), BF16)

    def count_coarse(cand_bf):
        def body(r, accs):
            blk = scb_ref[pl.ds(pl.multiple_of(r * kb, kb), kb), :]
            accs = list(accs)
            for g16 in range(kb // 16):
                a = accs[g16 % n_acc]
                accs[g16 % n_acc] = jnp.where(blk[g16 * 16:(g16 + 1) * 16] >= cand_bf, a + one_bf, a)
            return tuple(accs)
        accs = lax.fori_loop(0, nblk, body, (jnp.zeros((16, tq), BF16),) * n_acc)
        return jnp.sum(sum(a.astype(F32) for a in accs), axis=0, keepdims=True)

    def coarse_body(b, t_u):
        cand_u = t_u | lax.shift_left(jnp.int32(1), 31 - b)
        cnt = count_coarse(key_to_f32(cand_u).astype(BF16))
        return jnp.where(cnt >= topk, cand_u, t_u)

    p_u = lax.fori_loop(0, 16, coarse_body, jnp.zeros((1, tq), jnp.int32))

    lo_u = p_u - jnp.int32(0x8000)

    def fine_body(b, off):
        cand_off = off | lax.shift_left(jnp.int32(1), 16 - b)
        cand = key_to_f32(lo_u + cand_off)
        cnt = count(lambda blk, ks: blk >= cand)
        return jnp.where(cnt >= topk, cand_off, off)

    t_u = lo_u + lax.fori_loop(0, 17, fine_body, jnp.zeros((1, tq), jnp.int32))
    n_adm = (((i * tq + lax.broadcasted_iota(jnp.int32, (1, tq), 1)) >> CHUNK_SHIFT) + 1) << CHUNK_SHIFT
    thr = jnp.where(n_adm < topk, -jnp.inf, key_to_f32(t_u))
    n_ge = count(lambda blk, ks: blk >= thr)

    def tie_cut():
        need = topk - count(lambda blk, ks: blk > thr)

        def idx_body(b, j_u):
            cand = j_u | lax.shift_left(jnp.int32(1), idx_bits - 1 - b)
            cnt = count(lambda blk, ks: (blk == thr) & ((ks + krow) < cand))
            return jnp.where(cnt < need, cand, j_u)

        return lax.fori_loop(0, idx_bits, idx_body, jnp.zeros((1, tq), jnp.int32))

    def no_cut():
        return jnp.full((1, tq), 2 ** idx_bits - 1, jnp.int32)

    jcut = lax.cond(jnp.max(n_ge) > topk, tie_cut, no_cut)
    jcut = jnp.where(thr == -jnp.inf, -1, jcut)

    m_ref[...] = jnp.full(m_ref.shape, NEG, F32)
    acc_ref[...] = jnp.zeros_like(acc_ref)
    qg = [jnp.concatenate([q_ref[:, (g * rep + e) * C_HEAD_DIM:(g * rep + e + 1) * C_HEAD_DIM]
                           for e in range(rep)], axis=0) for g in range(C_KV_HEADS)]

    def scores(r, buf):
        ks = pl.multiple_of(r * kb, kb)
        for g in range(C_KV_HEADS):
            buf[g] = _dot_nt(k_ref[pl.ds(ks, kb), g * C_HEAD_DIM:(g + 1) * C_HEAD_DIM], qg[g])

    def consume(r, buf, is_last):
        ks = pl.multiple_of(r * kb, kb)
        sc = sc_ref[pl.ds(ks, kb), :]
        sel = (sc > thr) | ((sc == thr) & ((ks + krow) <= jcut))
        bias = jnp.where(sel, 0.0, NEG)
        bias = jnp.concatenate([bias] * rep, axis=1)
        for g in range(C_KV_HEADS):
            _softmax_step_t(buf[g] + bias, vt_ref[g, :, pl.ds(ks, kb)], m_ref.at[g], acc_ref.at[g])

    _pipelined_tiles(nblk, scores, consume, s0_ref, s1_ref)
    for g in range(C_KV_HEADS):
        a = acc_ref[g]
        ot = a[0:LANES] / a[LANES:LANES + 1]
        for e in range(rep):
            hh = g * rep + e
            o_ref[:, hh * C_HEAD_DIM:(hh + 1) * C_HEAD_DIM] = ot[:, e * tq:(e + 1) * tq].T.astype(BF16)


def _dsa(proj, iw_t, *, batch, seq, topk):
    t = batch * seq
    tq = min(256, seq)
    nq = seq // tq
    kvw = C_KV_HEADS * C_HEAD_DIM
    iqw = IDX_HEADS * IDX_DIM
    return pl.pallas_call(
        functools.partial(_dsa_kernel, tq=tq, kb=tq, seq=seq, topk=float(topk), idx_bits=(seq - 1).bit_length(),
                          idx_scale=IDX_HEADS ** -0.5 * IDX_DIM ** -0.5),
        grid=(batch, nq),
        in_specs=[
            pl.BlockSpec((tq, C_HEADS * C_HEAD_DIM), lambda b, i: (b * nq + i, 0)),
            pl.BlockSpec((seq, kvw), lambda b, i: (b, C_K0 // kvw)),
            pl.BlockSpec((seq, kvw), lambda b, i: (b, C_V0 // kvw)),
            pl.BlockSpec((tq, iqw), lambda b, i: (b * nq + i, C_IQ0 // iqw)),
            pl.BlockSpec((seq, LANES), lambda b, i: (b, C_IK0 // LANES)),
            pl.BlockSpec((1, 8, tq), lambda b, i: (b, 0, i)),
        ],
        out_specs=pl.BlockSpec((tq, C_HEADS * C_HEAD_DIM), lambda b, i: (b * nq + i, 0)),
        out_shape=jax.ShapeDtypeStruct((t, C_HEADS * C_HEAD_DIM), BF16),
        scratch_shapes=[pltpu.VMEM((seq, tq), F32), pltpu.VMEM((seq, tq), BF16),
                        pltpu.VMEM((C_KV_HEADS, V_ROWS, seq), BF16),
                        pltpu.VMEM((C_KV_HEADS, 1, C_HEADS // C_KV_HEADS * tq), F32),
                        pltpu.VMEM((C_KV_HEADS, V_ROWS, C_HEADS // C_KV_HEADS * tq), F32)]
                       + [pltpu.VMEM((C_KV_HEADS, tq, C_HEADS // C_KV_HEADS * tq), F32)] * 2,
        compiler_params=_cparams(("arbitrary", "arbitrary")),
        name="dsa",
    )(proj, proj, proj, proj, proj, iw_t)


def _pad_c_weight(w):
    iw = w[:, 2112:2120]
    ik = w[:, 2048:2112]
    pad = jnp.zeros((w.shape[0], C_COLS - C_IW0 - IDX_HEADS), w.dtype)
    return jnp.concatenate([w[:, :2048], ik, ik, iw, pad], axis=1).astype(BF16)


def kernel(x, norm1_g, norm2_g, a_w_in, a_q_norm_g, a_k_norm_g, a_lambda_q1, a_lambda_k1, a_lambda_q2,
           a_lambda_k2, a_subln_g, a_w_out, b_w_group, b_scale, c_w_in, c_q_norm_g, c_k_norm_g, c_w_out,
           mlp_w1, mlp_w2):
    batch, seq, d = x.shape
    depth = norm1_g.shape[0]
    h = x.reshape(batch * seq, d)
    row = lambda v: v.reshape(1, -1).astype(F32)
    for i in range(depth):
        m, j = i % N_MIXERS, i // N_MIXERS
        g1 = row(norm1_g[i])
        if m == 0:
            lambda_init = 0.8 - 0.6 * math.exp(-0.3 * i)
            gain = jnp.concatenate([jnp.tile(a_q_norm_g[j], 2 * A_HEADS) * (A_HEAD_DIM ** -0.5 * LOG2E),
                                    jnp.tile(a_k_norm_g[j], 2 * A_HEADS),
                                    jnp.ones((A_HEADS * A_V_DIM,), F32)]).reshape(1, -1)
            qkv = _proj(h, g1, a_w_in[j].astype(BF16), gain, n_norm_cols=2 * D_MODEL, seg=A_HEAD_DIM, aux=False)
            o = _attn_a(qkv, row(a_lambda_q1[j]), row(a_lambda_k1[j]), row(a_lambda_q2[j]), row(a_lambda_k2[j]),
                        a_subln_g[j].reshape(-1, 1).astype(F32), batch=batch, seq=seq, lambda_init=lambda_init)
            h = _outproj(o, a_w_out[j].astype(BF16), h)
        elif m == 1:
            h = _pool(h, g1, b_w_group[j].astype(BF16), row(b_scale[j]), batch=batch, seq=seq)
        else:
            gain = jnp.concatenate([jnp.tile(c_q_norm_g[j], C_HEADS) * (C_HEAD_DIM ** -0.5 * LOG2E),
                                    jnp.tile(c_k_norm_g[j], C_KV_HEADS),
                                    jnp.ones((C_COLS - C_V0,), F32)]).reshape(1, -1)
            proj, aux = _proj(h, g1, _pad_c_weight(c_w_in[j]), gain, n_norm_cols=C_V0, seg=C_HEAD_DIM, aux=True)
            iw_t = aux[:, :IDX_HEADS].reshape(batch, seq, IDX_HEADS).transpose(0, 2, 1)
            o = _dsa(proj, iw_t, batch=batch, seq=seq, topk=min(TOPK_MAX, seq // 4))
            h = _outproj(o, c_w_out[j].astype(BF16), h)
        h = _mlp(h, row(norm2_g[i]), mlp_w1, mlp_w2, i)
    return h.reshape(batch, seq, d)
```

```python
import functools
import math

import jax
import jax.numpy as jnp
from jax import lax
from jax.experimental import pallas as pl
from jax.experimental.pallas import tpu as pltpu

F32 = jnp.float32
BF16 = jnp.bfloat16

D_MODEL = 1024
N_MIXERS = 3
CHUNK = 64
CHUNK_SHIFT = 6
EPS = 1e-6
LANES = 128
MXU_DIM = 256

A_HEADS = 8
A_HEAD_DIM = 64
A_V_DIM = 128
POOL_WINDOWS = (2, 4, 8, 16)
POOL_GROUP = 256
POOL_HALO = 16
C_HEADS = 8
C_HEAD_DIM = 128
C_KV_HEADS = 2
IDX_HEADS = 8
IDX_DIM = 64
TOPK_MAX = 256

C_Q0, C_K0, C_V0, C_IQ0, C_IK0, C_IW0, C_COLS = 0, 1024, 1280, 1536, 2048, 2176, 2304

NEG = -1e30
LOG2E = math.log2(math.e)
V_ROWS = 128 + 16
VMEM_LIMIT = 56 * 1024 * 1024


def _cparams(sem):
    return pltpu.CompilerParams(dimension_semantics=sem, vmem_limit_bytes=VMEM_LIMIT)


def _rms_rows(x, g):
    ms = jnp.mean(x * x, axis=-1, keepdims=True)
    return (x * lax.rsqrt(ms + EPS)) * g


def _dot_nt(a, b):
    return lax.dot_general(a, b, (((1,), (1,)), ((), ())), preferred_element_type=F32)


def _proj_kernel(h_ref, g_ref, w_ref, gain_ref, seg_ref, o_ref, *aux_ref, n_norm_cols, seg):
    u = _rms_rows(h_ref[...], g_ref[...]).astype(BF16)
    n = w_ref.shape[1]
    for c0 in range(0, n, 2 * MXU_DIM):
        cw = min(2 * MXU_DIM, n - c0)
        acc = jnp.dot(u, w_ref[:, c0:c0 + cw], preferred_element_type=F32)
        for s0 in range(0, cw, MXU_DIM):
            cols = slice(c0 + s0, c0 + s0 + MXU_DIM)
            a = acc[:, s0:s0 + MXU_DIM]
            if c0 + s0 < n_norm_cols:
                sq = a * a
                hi = sq.astype(BF16)
                lo = (sq - hi.astype(F32)).astype(BF16)
                ssum = (jnp.dot(hi, seg_ref[...], preferred_element_type=F32)
                        + jnp.dot(lo, seg_ref[...], preferred_element_type=F32))
                a = (a * lax.rsqrt(ssum * (1.0 / seg) + EPS)) * gain_ref[:, cols]
            o_ref[:, cols] = a.astype(BF16)
        if aux_ref and c0 + cw == n:
            aux_ref[0][...] = acc[:, cw - LANES:]


def _proj(h, g, w, gain, *, n_norm_cols, seg, aux):
    t, d = h.shape
    n = w.shape[1]
    tm = min(512, t)
    assert n % MXU_DIM == 0 and n_norm_cols % MXU_DIM == 0 and t % tm == 0
    r = lax.broadcasted_iota(jnp.int32, (MXU_DIM, MXU_DIM), 0) // seg
    c = lax.broadcasted_iota(jnp.int32, (MXU_DIM, MXU_DIM), 1) // seg
    segm = (r == c).astype(BF16)
    out_shape = [jax.ShapeDtypeStruct((t, n), BF16)]
    out_specs = [pl.BlockSpec((tm, n), lambda i: (i, 0))]
    if aux:
        out_shape.append(jax.ShapeDtypeStruct((t, LANES), F32))
        out_specs.append(pl.BlockSpec((tm, LANES), lambda i: (i, 0)))
    res = pl.pallas_call(
        functools.partial(_proj_kernel, n_norm_cols=n_norm_cols, seg=seg),
        grid=(t // tm,),
        in_specs=[
            pl.BlockSpec((tm, d), lambda i: (i, 0)),
            pl.BlockSpec((1, d), lambda i: (0, 0)),
            pl.BlockSpec((d, n), lambda i: (0, 0)),
            pl.BlockSpec((1, n), lambda i: (0, 0)),
            pl.BlockSpec((MXU_DIM, MXU_DIM), lambda i: (0, 0)),
        ],
        out_specs=out_specs,
        out_shape=out_shape,
        compiler_params=_cparams(("parallel",)),
        name="proj",
    )(h, g, w, gain, segm)
    return res if aux else res[0]


def _outproj_kernel(o_ref, w_ref, h_ref, out_ref):
    out_ref[...] = h_ref[...] + jnp.dot(o_ref[...], w_ref[...], preferred_element_type=F32)


def _outproj(o, w, h):
    t, d = h.shape
    tm = min(1024, t)
    return pl.pallas_call(
        _outproj_kernel,
        grid=(t // tm,),
        in_specs=[
            pl.BlockSpec((tm, o.shape[1]), lambda i: (i, 0)),
            pl.BlockSpec(w.shape, lambda i: (0, 0)),
            pl.BlockSpec((tm, d), lambda i: (i, 0)),
        ],
        out_specs=pl.BlockSpec((tm, d), lambda i: (i, 0)),
        out_shape=jax.ShapeDtypeStruct((t, d), F32),
        compiler_params=_cparams(("parallel",)),
        name="outproj",
    )(o, w, h)


def _mlp_kernel(h_ref, g_ref, w1_ref, w2_ref, o_ref, u_ref, acc_ref):
    j = pl.program_id(1)

    @pl.when(j == 0)
    def _():
        u_ref[...] = _rms_rows(h_ref[...], g_ref[...]).astype(BF16)
        acc_ref[...] = jnp.zeros_like(acc_ref)

    hid = jnp.dot(u_ref[...], w1_ref[...].astype(BF16), preferred_element_type=F32)
    hid = jnp.square(jnp.maximum(hid, 0.0)).astype(BF16)
    acc_ref[...] += jnp.dot(hid, w2_ref[...].astype(BF16), preferred_element_type=F32)

    @pl.when(j == pl.num_programs(1) - 1)
    def _():
        o_ref[...] = h_ref[...] + acc_ref[...]


def _mlp(h, g, w1, w2, layer):
    t, d = h.shape
    f = w1.shape[2]
    tm = min(1024, t)
    tf = 1024
    return pl.pallas_call(
        _mlp_kernel,
        grid=(t // tm, f // tf),
        in_specs=[
            pl.BlockSpec((tm, d), lambda i, j: (i, 0)),
            pl.BlockSpec((1, d), lambda i, j: (0, 0)),
            pl.BlockSpec((None, d, tf), lambda i, j: (layer, 0, j)),
            pl.BlockSpec((None, tf, d), lambda i, j: (layer, j, 0)),
        ],
        out_specs=pl.BlockSpec((tm, d), lambda i, j: (i, 0)),
        out_shape=jax.ShapeDtypeStruct((t, d), F32),
        scratch_shapes=[pltpu.VMEM((tm, d), BF16), pltpu.VMEM((tm, d), F32)],
        compiler_params=_cparams(("parallel", "arbitrary")),
        name="mlp",
    )(h, g, w1, w2)


def _softmax_step_t(st, vt1, m_ref, acc_ref):
    m_prev = m_ref[...]
    m_next = jnp.maximum(m_prev, jnp.max(st, axis=0, keepdims=True))
    p = jnp.exp2(st - m_next).astype(BF16)
    alpha = jnp.exp2(m_prev - m_next)
    m_ref[...] = m_next
    acc_ref[...] = alpha * acc_ref[...] + jnp.dot(vt1, p, preferred_element_type=F32)


def _pipelined_tiles(n, scores, consume, finish, buf_a, buf_b):
    n_pairs = (n - 1) // 2
    scores(0, buf_a)

    def pair(t, c):
        scores(2 * t + 1, buf_b)
        consume(2 * t, buf_a, False)
        scores(2 * t + 2, buf_a)
        consume(2 * t + 1, buf_b, False)
        return c

    lax.fori_loop(0, n_pairs, pair, 0)

    @pl.when(n - 2 * n_pairs == 1)
    def _():
        consume(n - 1, buf_a, True)
        finish()

    @pl.when(n - 2 * n_pairs == 2)
    def _():
        scores(n - 1, buf_b)
        consume(n - 2, buf_a, False)
        consume(n - 1, buf_b, True)
        finish()


def _store_vt(vt_ref, v_ref, cols, seq, blk):
    for c in range(seq // blk):
        rows = slice(c * blk, (c + 1) * blk)
        vt_ref[0:LANES, rows] = v_ref[rows, cols].astype(F32).T.astype(BF16)
    vt_ref[LANES:, :] = jnp.ones((V_ROWS - LANES, seq), BF16)


A_HEADS_PER_STEP = 4


def _attn_a_kernel(q_ref, k_ref, v_ref, lq1_ref, lk1_ref, lq2_ref, lk2_ref, subg_ref, o_ref,
                   vt_ref, m_ref, acc_ref, s0_ref, s1_ref, *, tq, tk, seq, lambda_init):
    i = pl.program_id(2)
    heads = range(A_HEADS_PER_STEP)
    hcols = lambda hh: slice(hh * LANES, (hh + 1) * LANES)

    @pl.when(i == 0)
    def _():
        for hh in heads:
            _store_vt(vt_ref.at[hh], v_ref, hcols(hh), seq, tk)

    lane = lax.broadcasted_iota(jnp.int32, (tq, LANES), 1)
    qm = []
    for hh in heads:
        q = q_ref[:, hcols(hh)].astype(F32)
        qm.append(jnp.concatenate([jnp.where(lane < A_HEAD_DIM, q, 0.0),
                                   jnp.where(lane >= A_HEAD_DIM, q, 0.0)], axis=0).astype(BF16))
    m_ref[...] = jnp.full(m_ref.shape, NEG, F32)
    acc_ref[...] = jnp.zeros_like(acc_ref)

    def scores(j, buf):
        ks = pl.multiple_of(j * tk, tk)
        for hh in heads:
            buf[hh] = _dot_nt(k_ref[pl.ds(ks, tk), hcols(hh)], qm[hh])

    def consume(j, buf, masked):
        ks = pl.multiple_of(j * tk, tk)
        if masked:
            kpos = ks + lax.broadcasted_iota(jnp.int32, (tk, 2 * tq), 0)
            qpos = i * tq + (lax.broadcasted_iota(jnp.int32, (tk, 2 * tq), 1) & (tq - 1))
            ok = (kpos >> CHUNK_SHIFT) <= (qpos >> CHUNK_SHIFT)
        for hh in heads:
            st = buf[hh]
            if masked:
                st = jnp.where(ok, st, NEG)
            _softmax_step_t(st, vt_ref[hh, :, pl.ds(ks, tk)], m_ref.at[hh], acc_ref.at[hh])

    def finish():
        lam = (jnp.exp(jnp.sum(lq1_ref[...] * lk1_ref[...], axis=1, keepdims=True))
               - jnp.exp(jnp.sum(lq2_ref[...] * lk2_ref[...], axis=1, keepdims=True)) + lambda_init)
        for hh in heads:
            acc = acc_ref[hh]
            ot = acc[0:LANES] / acc[LANES:LANES + 1]
            od = ot[:, :tq] - lam * ot[:, tq:]
            ms = jnp.mean(od * od, axis=0, keepdims=True)
            y = (od * lax.rsqrt(ms + EPS)) * subg_ref[...] * (1.0 - lambda_init)
            o_ref[:, hcols(hh)] = y.T.astype(BF16)

    _pipelined_tiles(((i + 1) * tq + tk - 1) // tk, scores, consume, finish, s0_ref, s1_ref)


def _attn_a(qkv, lq1, lk1, lq2, lk2, subg, *, batch, seq, lambda_init):
    t = batch * seq
    tq = min(512, seq)
    tk = min(512, seq)
    nq = seq // tq
    hp = A_HEADS_PER_STEP
    ngrp = A_HEADS // hp
    vec = lambda n: pl.BlockSpec((1, n), lambda b, h, i: (0, 0))
    return pl.pallas_call(
        functools.partial(_attn_a_kernel, tq=tq, tk=tk, seq=seq, lambda_init=lambda_init),
        grid=(batch, ngrp, nq),
        in_specs=[
            pl.BlockSpec((tq, hp * LANES), lambda b, h, i: (b * nq + i, h)),
            pl.BlockSpec((seq, hp * LANES), lambda b, h, i: (b, ngrp + h)),
            pl.BlockSpec((seq, hp * LANES), lambda b, h, i: (b, 2 * ngrp + h)),
            vec(A_HEAD_DIM), vec(A_HEAD_DIM), vec(A_HEAD_DIM), vec(A_HEAD_DIM),
            pl.BlockSpec((A_V_DIM, 1), lambda b, h, i: (0, 0)),
        ],
        out_specs=pl.BlockSpec((tq, hp * LANES), lambda b, h, i: (b * nq + i, h)),
        out_shape=jax.ShapeDtypeStruct((t, A_HEADS * A_V_DIM), BF16),
        scratch_shapes=[pltpu.VMEM((hp, V_ROWS, seq), BF16), pltpu.VMEM((hp, 1, 2 * tq), F32),
                        pltpu.VMEM((hp, V_ROWS, 2 * tq), F32)] + [pltpu.VMEM((hp, tk, 2 * tq), F32)] * 2,
        compiler_params=_cparams(("arbitrary", "arbitrary", "arbitrary")),
        name="attn_a",
    )(qkv, qkv, qkv, lq1, lk1, lq2, lk2, subg)


def _pool_kernel(h_ref, halo_ref, g_ref, wg_ref, sc_ref, o_ref, ext_ref, *, ts):
    i = pl.program_id(1)
    x = h_ref[...]
    u = _rms_rows(x, g_ref[...])
    uh = _rms_rows(halo_ref[...], g_ref[...])
    ext_ref[0:POOL_HALO, :] = jnp.where(i > 0, uh, 0.0)
    ext_ref[POOL_HALO:, :] = u
    pos = i * ts + lax.broadcasted_iota(jnp.int32, (ts, 1), 0)
    for g, w in enumerate(POOL_WINDOWS):
        cols = slice(g * POOL_GROUP, (g + 1) * POOL_GROUP)
        win = ext_ref[POOL_HALO:, cols]
        for d in range(1, w):
            win = win + ext_ref[pl.ds(POOL_HALO - d, ts), cols]
        count = jnp.minimum(pos + 1, w).astype(F32)
        pooled = win / count - u[:, cols]
        y = jnp.dot(pooled.astype(BF16), wg_ref[g], preferred_element_type=F32)
        o_ref[:, cols] = x[:, cols] + y * sc_ref[:, cols]


def _pool(h, g, wg, scale, *, batch, seq):
    t, d = h.shape
    ts = min(512, seq)
    ns = seq // ts
    hb = ts // POOL_HALO
    return pl.pallas_call(
        functools.partial(_pool_kernel, ts=ts),
        grid=(batch, ns),
        in_specs=[
            pl.BlockSpec((ts, d), lambda b, i: (b * ns + i, 0)),
            pl.BlockSpec((POOL_HALO, d), lambda b, i: (jnp.maximum((b * ns + i) * hb - 1, 0), 0)),
            pl.BlockSpec((1, d), lambda b, i: (0, 0)),
            pl.BlockSpec(wg.shape, lambda b, i: (0, 0, 0)),
            pl.BlockSpec((1, d), lambda b, i: (0, 0)),
        ],
        out_specs=pl.BlockSpec((ts, d), lambda b, i: (b * ns + i, 0)),
        out_shape=jax.ShapeDtypeStruct((t, d), F32),
        scratch_shapes=[pltpu.VMEM((ts + POOL_HALO, d), F32)],
        compiler_params=_cparams(("parallel", "arbitrary")),
        name="pool",
    )(h, h, g, wg, scale)


def _masked_heads(x, n_pairs, rows):
    xf = x.astype(F32)
    lane = lax.broadcasted_iota(jnp.int32, (rows, LANES), 1)
    out = []
    for p in range(n_pairs):
        xp = xf[:, p * LANES:(p + 1) * LANES]
        out.append(jnp.where(lane < IDX_DIM, xp, 0.0).astype(BF16))
        out.append(jnp.where(lane >= IDX_DIM, xp, 0.0).astype(BF16))
    return out


def _sum_sublane_groups(m, rows):
    parts = [m[r:r + 8] for r in range(0, rows, 8)]
    while len(parts) > 1:
        parts = [parts[a] + parts[a + 1] for a in range(0, len(parts), 2)]
    return parts[0]


def _dsa_kernel(q_ref, k_ref, v_ref, iq_ref, ik_ref, iw_ref, o_ref, sc_ref, scb_ref, vt_ref, m_ref, acc_ref,
                s0_ref, s1_ref, *, tq, kb, seq, topk, idx_bits, idx_scale):
    i = pl.program_id(1)
    rep = C_HEADS // C_KV_HEADS

    @pl.when(i == 0)
    def _():
        for g in range(C_KV_HEADS):
            _store_vt(vt_ref.at[g], v_ref, slice(g * C_HEAD_DIM, (g + 1) * C_HEAD_DIM), seq, kb)

    nblk = ((i + 1) * tq) // kb
    iw = iw_ref[0] * idx_scale
    iqh = _masked_heads(iq_ref[...], IDX_HEADS // 2, tq)
    qchunk = (i * tq + lax.broadcasted_iota(jnp.int32, (kb, tq), 1)) >> CHUNK_SHIFT
    krow = lax.broadcasted_iota(jnp.int32, (kb, tq), 0)

    def score_block(r):
        ks = pl.multiple_of(r * kb, kb)
        ikb = ik_ref[pl.ds(ks, kb), :]
        acc = jnp.zeros((kb, tq), F32)
        for h in range(IDX_HEADS):
            acc = acc + iw[h:h + 1, :] * jnp.maximum(_dot_nt(ikb, iqh[h]), 0.0)
        acc = jnp.where(((ks + krow) >> CHUNK_SHIFT) <= qchunk, acc, -jnp.inf)
        sc_ref[pl.ds(ks, kb), :] = acc
        scb_ref[pl.ds(ks, kb), :] = acc.astype(BF16)

    def score_pair(t, c):
        score_block(2 * t)
        score_block(2 * t + 1)
        return c

    lax.fori_loop(0, nblk // 2, score_pair, 0)

    @pl.when(nblk % 2 == 1)
    def _():
        score_block(nblk - 1)

    n_acc = 4

    def count(pred):
        def body(r, accs):
            ks = pl.multiple_of(r * kb, kb)
            p = pred(sc_ref[pl.ds(ks, kb), :], ks)
            accs = list(accs)
            for g8 in range(kb // 8):
                a = accs[g8 % n_acc]
                accs[g8 % n_acc] = jnp.where(p[g8 * 8:(g8 + 1) * 8], a + 1.0, a)
            return tuple(accs)
        accs = lax.fori_loop(0, nblk, body, (jnp.zeros((8, tq), F32),) * n_acc)
        return jnp.sum(sum(accs), axis=0, keepdims=True)

    int_min = jnp.int32(-2 ** 31)

    def key_to_f32(key_u):
        ks_ = key_u ^ int_min
        bits = jnp.where(ks_ >= 0, ks_, ks_ ^ jnp.int32(0x7FFFFFFF))
        return lax.bitcast_convert_type(bits, F32)

    one_bf = jnp.ones((), BF16)

    def count_coarse(cand_bf):
        def body(r, accs):
            blk = scb_ref[pl.ds(pl.multiple_of(r * kb, kb), kb), :]
            accs = list(accs)
            for g16 in range(kb // 16):
                a = accs[g16 % n_acc]
                accs[g16 % n_acc] = jnp.where(blk[g16 * 16:(g16 + 1) * 16] >= cand_bf, a + one_bf, a)
            return tuple(accs)
        accs = lax.fori_loop(0, nblk, body, (jnp.zeros((16, tq), BF16),) * n_acc)
        return jnp.sum(sum(a.astype(F32) for a in accs), axis=0, keepdims=True)

    def coarse_body(b, t_u):
        cand_u = t_u | lax.shift_left(jnp.int32(1), 31 - b)
        cnt = count_coarse(key_to_f32(cand_u).astype(BF16))
        return jnp.where(cnt >= topk, cand_u, t_u)

    p_u = lax.fori_loop(0, 16, coarse_body, jnp.zeros((1, tq), jnp.int32))

    lo_u = p_u - jnp.int32(0x10000)

    def fine_body(b, off):
        cand_off = off | lax.shift_left(jnp.int32(1), 16 - b)
        cand = key_to_f32(lo_u + cand_off)
        cnt = count(lambda blk, ks: blk >= cand)
        return jnp.where(cnt >= topk, cand_off, off)

    t_u = lo_u + lax.fori_loop(0, 17, fine_body, jnp.zeros((1, tq), jnp.int32))
    n_adm = (((i * tq + lax.broadcasted_iota(jnp.int32, (1, tq), 1)) >> CHUNK_SHIFT) + 1) << CHUNK_SHIFT
    thr = jnp.where(n_adm < topk, -jnp.inf, key_to_f32(t_u))
    n_ge = count(lambda blk, ks: blk >= thr)

    def tie_cut():
        need = topk - count(lambda blk, ks: blk > thr)

        def idx_body(b, j_u):
            cand = j_u | lax.shift_left(jnp.int32(1), idx_bits - 1 - b)
            cnt = count(lambda blk, ks: (blk == thr) & ((ks + krow) < cand))
            return jnp.where(cnt < need, cand, j_u)

        return lax.fori_loop(0, idx_bits, idx_body, jnp.zeros((1, tq), jnp.int32))

    def no_cut():
        return jnp.full((1, tq), 2 ** idx_bits - 1, jnp.int32)

    jcut = lax.cond(jnp.max(n_ge) > topk, tie_cut, no_cut)
    jcut = jnp.where(thr == -jnp.inf, -1, jcut)

    m_ref[...] = jnp.full(m_ref.shape, NEG, F32)
    acc_ref[...] = jnp.zeros_like(acc_ref)
    qg = [jnp.concatenate([q_ref[:, (g * rep + e) * C_HEAD_DIM:(g * rep + e + 1) * C_HEAD_DIM]
                           for e in range(rep)], axis=0) for g in range(C_KV_HEADS)]

    def scores(r, buf):
        ks = pl.multiple_of(r * kb, kb)
        for g in range(C_KV_HEADS):
            buf[g] = _dot_nt(k_ref[pl.ds(ks, kb), g * C_HEAD_DIM:(g + 1) * C_HEAD_DIM], qg[g])

    def consume(r, buf, is_last):
        ks = pl.multiple_of(r * kb, kb)
        sc = sc_ref[pl.ds(ks, kb), :]
        sel = (sc > thr) | ((sc == thr) & ((ks + krow) <= jcut))
        bias = jnp.where(sel, 0.0, NEG)
        bias = jnp.concatenate([bias] * rep, axis=1)
        for g in range(C_KV_HEADS):
            _softmax_step_t(buf[g] + bias, vt_ref[g, :, pl.ds(ks, kb)], m_ref.at[g], acc_ref.at[g])

    def finish():
        for g in range(C_KV_HEADS):
            a = acc_ref[g]
            ot = a[0:LANES] / a[LANES:LANES + 1]
            for e in range(rep):
                hh = g * rep + e
                o_ref[:, hh * C_HEAD_DIM:(hh + 1) * C_HEAD_DIM] = ot[:, e * tq:(e + 1) * tq].T.astype(BF16)

    _pipelined_tiles(nblk, scores, consume, finish, s0_ref, s1_ref)


def _dsa(proj, iw_t, *, batch, seq, topk):
    t = batch * seq
    tq = min(256, seq)
    nq = seq // tq
    kvw = C_KV_HEADS * C_HEAD_DIM
    iqw = IDX_HEADS * IDX_DIM
    return pl.pallas_call(
        functools.partial(_dsa_kernel, tq=tq, kb=tq, seq=seq, topk=float(topk), idx_bits=(seq - 1).bit_length(),
                          idx_scale=IDX_HEADS ** -0.5 * IDX_DIM ** -0.5),
        grid=(batch, nq),
        in_specs=[
            pl.BlockSpec((tq, C_HEADS * C_HEAD_DIM), lambda b, i: (b * nq + i, 0)),
            pl.BlockSpec((seq, kvw), lambda b, i: (b, C_K0 // kvw)),
            pl.BlockSpec((seq, kvw), lambda b, i: (b, C_V0 // kvw)),
            pl.BlockSpec((tq, iqw), lambda b, i: (b * nq + i, C_IQ0 // iqw)),
            pl.BlockSpec((seq, LANES), lambda b, i: (b, C_IK0 // LANES)),
            pl.BlockSpec((1, 8, tq), lambda b, i: (b, 0, i)),
        ],
        out_specs=pl.BlockSpec((tq, C_HEADS * C_HEAD_DIM), lambda b, i: (b * nq + i, 0)),
        out_shape=jax.ShapeDtypeStruct((t, C_HEADS * C_HEAD_DIM), BF16),
        scratch_shapes=[pltpu.VMEM((seq, tq), F32), pltpu.VMEM((seq, tq), BF16),
                        pltpu.VMEM((C_KV_HEADS, V_ROWS, seq), BF16),
                        pltpu.VMEM((C_KV_HEADS, 1, C_HEADS // C_KV_HEADS * tq), F32),
                        pltpu.VMEM((C_KV_HEADS, V_ROWS, C_HEADS // C_KV_HEADS * tq), F32)]
                       + [pltpu.VMEM((C_KV_HEADS, tq, C_HEADS // C_KV_HEADS * tq), F32)] * 2,
        compiler_params=_cparams(("arbitrary", "arbitrary")),
        name="dsa",
    )(proj, proj, proj, proj, proj, iw_t)


def _pad_c_weight(w):
    iw = w[:, 2112:2120]
    ik = w[:, 2048:2112]
    pad = jnp.zeros((w.shape[0], C_COLS - C_IW0 - IDX_HEADS), w.dtype)
    return jnp.concatenate([w[:, :2048], ik, ik, iw, pad], axis=1).astype(BF16)


def kernel(x, norm1_g, norm2_g, a_w_in, a_q_norm_g, a_k_norm_g, a_lambda_q1, a_lambda_k1, a_lambda_q2,
           a_lambda_k2, a_subln_g, a_w_out, b_w_group, b_scale, c_w_in, c_q_norm_g, c_k_norm_g, c_w_out,
           mlp_w1, mlp_w2):
    batch, seq, d = x.shape
    depth = norm1_g.shape[0]
    h = x.reshape(batch * seq, d)
    row = lambda v: v.reshape(1, -1).astype(F32)
    for i in range(depth):
        m, j = i % N_MIXERS, i // N_MIXERS
        g1 = row(norm1_g[i])
        if m == 0:
            lambda_init = 0.8 - 0.6 * math.exp(-0.3 * i)
            gain = jnp.concatenate([jnp.tile(a_q_norm_g[j], 2 * A_HEADS) * (A_HEAD_DIM ** -0.5 * LOG2E),
                                    jnp.tile(a_k_norm_g[j], 2 * A_HEADS),
                                    jnp.ones((A_HEADS * A_V_DIM,), F32)]).reshape(1, -1)
            qkv = _proj(h, g1, a_w_in[j].astype(BF16), gain, n_norm_cols=2 * D_MODEL, seg=A_HEAD_DIM, aux=False)
            o = _attn_a(qkv, row(a_lambda_q1[j]), row(a_lambda_k1[j]), row(a_lambda_q2[j]), row(a_lambda_k2[j]),
                        a_subln_g[j].reshape(-1, 1).astype(F32), batch=batch, seq=seq, lambda_init=lambda_init)
            h = _outproj(o, a_w_out[j].astype(BF16), h)
        elif m == 1:
            h = _pool(h, g1, b_w_group[j].astype(BF16), row(b_scale[j]), batch=batch, seq=seq)
        else:
            gain = jnp.concatenate([jnp.tile(c_q_norm_g[j], C_HEADS) * (C_HEAD_DIM ** -0.5 * LOG2E),
                                    jnp.tile(c_k_norm_g[j], C_KV_HEADS),
                                    jnp.ones((C_COLS - C_V0,), F32)]).reshape(1, -1)
            proj, aux = _proj(h, g1, _pad_c_weight(c_w_in[j]), gain, n_norm_cols=C_V0, seg=C_HEAD_DIM, aux=True)
            iw_t = aux[:, :IDX_HEADS].reshape(batch, seq, IDX_HEADS).transpose(0, 2, 1)
            o = _dsa(proj, iw_t, batch=batch, seq=seq, topk=min(TOPK_MAX, seq // 4))
            h = _outproj(o, c_w_out[j].astype(BF16), h)
        h = _mlp(h, row(norm2_g[i]), mlp_w1, mlp_w2, i)
    return h.reshape(batch, seq, d)
```

```python
import functools
import math

import jax
import jax.numpy as jnp
from jax import lax
from jax.experimental import pallas as pl
from jax.experimental.pallas import tpu as pltpu

F32 = jnp.float32
BF16 = jnp.bfloat16

D_MODEL = 1024
N_MIXERS = 3
CHUNK = 64
CHUNK_SHIFT = 6
EPS = 1e-6
LANES = 128
MXU_DIM = 256

A_HEADS = 8
A_HEAD_DIM = 64
A_V_DIM = 128
POOL_WINDOWS = (2, 4, 8, 16)
POOL_GROUP = 256
POOL_HALO = 16
C_HEADS = 8
C_HEAD_DIM = 128
C_KV_HEADS = 2
IDX_HEADS = 8
IDX_DIM = 64
TOPK_MAX = 256

C_Q0, C_K0, C_V0, C_IQ0, C_IK0, C_IW0, C_COLS = 0, 1024, 1280, 1536, 2048, 2176, 2304

NEG = -1e30
LOG2E = math.log2(math.e)
V_ROWS = 128 + 16
VMEM_LIMIT = 56 * 1024 * 1024


def _cparams(sem):
    return pltpu.CompilerParams(dimension_semantics=sem, vmem_limit_bytes=VMEM_LIMIT)


def _rms_rows(x, g):
    ms = jnp.mean(x * x, axis=-1, keepdims=True)
    return (x * lax.rsqrt(ms + EPS)) * g


def _dot_nt(a, b):
    return lax.dot_general(a, b, (((1,), (1,)), ((), ())), preferred_element_type=F32)


def _proj_kernel(h_ref, g_ref, w_ref, gain_ref, seg_ref, o_ref, *aux_ref, n_norm_cols, seg):
    u = _rms_rows(h_ref[...], g_ref[...]).astype(BF16)
    n = w_ref.shape[1]
    for c0 in range(0, n, 2 * MXU_DIM):
        cw = min(2 * MXU_DIM, n - c0)
        acc = jnp.dot(u, w_ref[:, c0:c0 + cw], preferred_element_type=F32)
        for s0 in range(0, cw, MXU_DIM):
            cols = slice(c0 + s0, c0 + s0 + MXU_DIM)
            a = acc[:, s0:s0 + MXU_DIM]
            if c0 + s0 < n_norm_cols:
                sq = a * a
                hi = sq.astype(BF16)
                lo = (sq - hi.astype(F32)).astype(BF16)
                ssum = (jnp.dot(hi, seg_ref[...], preferred_element_type=F32)
                        + jnp.dot(lo, seg_ref[...], preferred_element_type=F32))
                a = (a * lax.rsqrt(ssum * (1.0 / seg) + EPS)) * gain_ref[:, cols]
            o_ref[:, cols] = a.astype(BF16)
        if aux_ref and c0 + cw == n:
            aux_ref[0][...] = acc[:, cw - LANES:]


def _proj(h, g, w, gain, *, n_norm_cols, seg, aux):
    t, d = h.shape
    n = w.shape[1]
    tm = min(512, t)
    assert n % MXU_DIM == 0 and n_norm_cols % MXU_DIM == 0 and t % tm == 0
    r = lax.broadcasted_iota(jnp.int32, (MXU_DIM, MXU_DIM), 0) // seg
    c = lax.broadcasted_iota(jnp.int32, (MXU_DIM, MXU_DIM), 1) // seg
    segm = (r == c).astype(BF16)
    out_shape = [jax.ShapeDtypeStruct((t, n), BF16)]
    out_specs = [pl.BlockSpec((tm, n), lambda i: (i, 0))]
    if aux:
        out_shape.append(jax.ShapeDtypeStruct((t, LANES), F32))
        out_specs.append(pl.BlockSpec((tm, LANES), lambda i: (i, 0)))
    res = pl.pallas_call(
        functools.partial(_proj_kernel, n_norm_cols=n_norm_cols, seg=seg),
        grid=(t // tm,),
        in_specs=[
            pl.BlockSpec((tm, d), lambda i: (i, 0)),
            pl.BlockSpec((1, d), lambda i: (0, 0)),
            pl.BlockSpec((d, n), lambda i: (0, 0)),
            pl.BlockSpec((1, n), lambda i: (0, 0)),
            pl.BlockSpec((MXU_DIM, MXU_DIM), lambda i: (0, 0)),
        ],
        out_specs=out_specs,
        out_shape=out_shape,
        compiler_params=_cparams(("parallel",)),
        name="proj",
    )(h, g, w, gain, segm)
    return res if aux else res[0]


def _outproj_kernel(o_ref, w_ref, h_ref, out_ref):
    out_ref[...] = h_ref[...] + jnp.dot(o_ref[...], w_ref[...], preferred_element_type=F32)


def _outproj(o, w, h):
    t, d = h.shape
    tm = min(1024, t)
    return pl.pallas_call(
        _outproj_kernel,
        grid=(t // tm,),
        in_specs=[
            pl.BlockSpec((tm, o.shape[1]), lambda i: (i, 0)),
            pl.BlockSpec(w.shape, lambda i: (0, 0)),
            pl.BlockSpec((tm, d), lambda i: (i, 0)),
        ],
        out_specs=pl.BlockSpec((tm, d), lambda i: (i, 0)),
        out_shape=jax.ShapeDtypeStruct((t, d), F32),
        compiler_params=_cparams(("parallel",)),
        name="outproj",
    )(o, w, h)


def _mlp_kernel(h_ref, g_ref, w1_ref, w2_ref, o_ref, u_ref, acc_ref):
    j = pl.program_id(1)

    @pl.when(j == 0)
    def _():
        u_ref[...] = _rms_rows(h_ref[...], g_ref[...]).astype(BF16)
        acc_ref[...] = jnp.zeros_like(acc_ref)

    hid = jnp.dot(u_ref[...], w1_ref[...].astype(BF16), preferred_element_type=F32)
    hid = jnp.square(jnp.maximum(hid, 0.0)).astype(BF16)
    acc_ref[...] += jnp.dot(hid, w2_ref[...].astype(BF16), preferred_element_type=F32)

    @pl.when(j == pl.num_programs(1) - 1)
    def _():
        o_ref[...] = h_ref[...] + acc_ref[...]


def _mlp(h, g, w1, w2, layer):
    t, d = h.shape
    f = w1.shape[2]
    tm = min(1024, t)
    tf = 1024
    return pl.pallas_call(
        _mlp_kernel,
        grid=(t // tm, f // tf),
        in_specs=[
            pl.BlockSpec((tm, d), lambda i, j: (i, 0)),
            pl.BlockSpec((1, d), lambda i, j: (0, 0)),
            pl.BlockSpec((None, d, tf), lambda i, j: (layer, 0, j)),
            pl.BlockSpec((None, tf, d), lambda i, j: (layer, j, 0)),
        ],
        out_specs=pl.BlockSpec((tm, d), lambda i, j: (i, 0)),
        out_shape=jax.ShapeDtypeStruct((t, d), F32),
        scratch_shapes=[pltpu.VMEM((tm, d), BF16), pltpu.VMEM((tm, d), F32)],
        compiler_params=_cparams(("parallel", "arbitrary")),
        name="mlp",
    )(h, g, w1, w2)


def _softmax_step_t(st, vt1, m_ref, acc_ref):
    m_prev = m_ref[...]
    m_next = jnp.maximum(m_prev, jnp.max(st, axis=0, keepdims=True))
    p = jnp.exp2(st - m_next).astype(BF16)
    alpha = jnp.exp2(m_prev - m_next)
    m_ref[...] = m_next
    acc_ref[...] = alpha * acc_ref[...] + jnp.dot(vt1, p, preferred_element_type=F32)


def _pipelined_tiles(n, scores, consume, finish, buf_a, buf_b):
    n_pairs = (n - 1) // 2

    def run(parts):
        for part in parts:
            part()

    def both(j_scores, buf_scores, j_consume, buf_consume):
        for s_part, c_part in zip(scores(j_scores, buf_scores), consume(j_consume, buf_consume, False)):
            s_part()
            c_part()

    run(scores(0, buf_a))

    def pair(t, c):
        both(2 * t + 1, buf_b, 2 * t, buf_a)
        both(2 * t + 2, buf_a, 2 * t + 1, buf_b)
        return c

    lax.fori_loop(0, n_pairs, pair, 0)

    @pl.when(n - 2 * n_pairs == 1)
    def _():
        run(consume(n - 1, buf_a, True))
        finish()

    @pl.when(n - 2 * n_pairs == 2)
    def _():
        both(n - 1, buf_b, n - 2, buf_a)
        run(consume(n - 1, buf_b, True))
        finish()


def _store_vt(vt_ref, v_ref, cols, seq, blk):
    for c in range(seq // blk):
        rows = slice(c * blk, (c + 1) * blk)
        vt_ref[0:LANES, rows] = v_ref[rows, cols].astype(F32).T.astype(BF16)
    vt_ref[LANES:, :] = jnp.ones((V_ROWS - LANES, seq), BF16)


A_HEADS_PER_STEP = 4


def _attn_a_kernel(q_ref, k_ref, v_ref, lq1_ref, lk1_ref, lq2_ref, lk2_ref, subg_ref, o_ref,
                   vt_ref, qm_ref, m_ref, acc_ref, s0_ref, s1_ref, *, tq, tk, seq, lambda_init):
    i = pl.program_id(2)
    heads = range(A_HEADS_PER_STEP)
    hcols = lambda hh: slice(hh * LANES, (hh + 1) * LANES)

    @pl.when(i == 0)
    def _():
        for hh in heads:
            _store_vt(vt_ref.at[hh], v_ref, hcols(hh), seq, tk)

    lane = lax.broadcasted_iota(jnp.int32, (tq, LANES), 1)
    for hh in heads:
        q = q_ref[:, hcols(hh)].astype(F32)
        qm_ref[hh, 0:tq, :] = jnp.where(lane < A_HEAD_DIM, q, 0.0).astype(BF16)
        qm_ref[hh, tq:, :] = jnp.where(lane >= A_HEAD_DIM, q, 0.0).astype(BF16)
    m_ref[...] = jnp.full(m_ref.shape, NEG, F32)
    acc_ref[...] = jnp.zeros_like(acc_ref)

    parts = [(hh, slice(mp * tq, (mp + 1) * tq)) for hh in heads for mp in range(2)]

    def scores(j, buf):
        ks = pl.multiple_of(j * tk, tk)

        def part(hh, lanes):
            buf[hh, :, lanes] = _dot_nt(k_ref[pl.ds(ks, tk), hcols(hh)], qm_ref[hh, lanes, :])

        return [functools.partial(part, hh, lanes) for hh, lanes in parts]

    def consume(j, buf, masked):
        ks = pl.multiple_of(j * tk, tk)
        if masked:
            kpos = ks + lax.broadcasted_iota(jnp.int32, (tk, tq), 0)
            qpos = i * tq + lax.broadcasted_iota(jnp.int32, (tk, tq), 1)
            ok = (kpos >> CHUNK_SHIFT) <= (qpos >> CHUNK_SHIFT)

        def part(hh, lanes):
            st = buf[hh, :, lanes]
            if masked:
                st = jnp.where(ok, st, NEG)
            _softmax_step_t(st, vt_ref[hh, :, pl.ds(ks, tk)], m_ref.at[hh, :, lanes], acc_ref.at[hh, :, lanes])

        return [functools.partial(part, hh, lanes) for hh, lanes in parts]

    def finish():
        lam = (jnp.exp(jnp.sum(lq1_ref[...] * lk1_ref[...], axis=1, keepdims=True))
               - jnp.exp(jnp.sum(lq2_ref[...] * lk2_ref[...], axis=1, keepdims=True)) + lambda_init)
        for hh in heads:
            acc = acc_ref[hh]
            ot = acc[0:LANES] / acc[LANES:LANES + 1]
            od = ot[:, :tq] - lam * ot[:, tq:]
            ms = jnp.mean(od * od, axis=0, keepdims=True)
            y = (od * lax.rsqrt(ms + EPS)) * subg_ref[...] * (1.0 - lambda_init)
            o_ref[:, hcols(hh)] = y.T.astype(BF16)

    _pipelined_tiles(((i + 1) * tq + tk - 1) // tk, scores, consume, finish, s0_ref, s1_ref)


def _attn_a(qkv, lq1, lk1, lq2, lk2, subg, *, batch, seq, lambda_init):
    t = batch * seq
    tq = min(512, seq)
    tk = min(512, seq)
    nq = seq // tq
    hp = A_HEADS_PER_STEP
    ngrp = A_HEADS // hp
    vec = lambda n: pl.BlockSpec((1, n), lambda b, h, i: (0, 0))
    return pl.pallas_call(
        functools.partial(_attn_a_kernel, tq=tq, tk=tk, seq=seq, lambda_init=lambda_init),
        grid=(batch, ngrp, nq),
        in_specs=[
            pl.BlockSpec((tq, hp * LANES), lambda b, h, i: (b * nq + i, h)),
            pl.BlockSpec((seq, hp * LANES), lambda b, h, i: (b, ngrp + h)),
            pl.BlockSpec((seq, hp * LANES), lambda b, h, i: (b, 2 * ngrp + h)),
            vec(A_HEAD_DIM), vec(A_HEAD_DIM), vec(A_HEAD_DIM), vec(A_HEAD_DIM),
            pl.BlockSpec((A_V_DIM, 1), lambda b, h, i: (0, 0)),
        ],
        out_specs=pl.BlockSpec((tq, hp * LANES), lambda b, h, i: (b * nq + i, h)),
        out_shape=jax.ShapeDtypeStruct((t, A_HEADS * A_V_DIM), BF16),
        scratch_shapes=[pltpu.VMEM((hp, V_ROWS, seq), BF16), pltpu.VMEM((hp, 2 * tq, LANES), BF16),
                        pltpu.VMEM((hp, 1, 2 * tq), F32),
                        pltpu.VMEM((hp, V_ROWS, 2 * tq), F32)] + [pltpu.VMEM((hp, tk, 2 * tq), F32)] * 2,
        compiler_params=_cparams(("arbitrary", "arbitrary", "arbitrary")),
        name="attn_a",
    )(qkv, qkv, qkv, lq1, lk1, lq2, lk2, subg)


def _pool_kernel(h_ref, halo_ref, g_ref, wg_ref, sc_ref, o_ref, ext_ref, *, ts):
    i = pl.program_id(1)
    x = h_ref[...]
    u = _rms_rows(x, g_ref[...])
    uh = _rms_rows(halo_ref[...], g_ref[...])
    ext_ref[0:POOL_HALO, :] = jnp.where(i > 0, uh, 0.0)
    ext_ref[POOL_HALO:, :] = u
    pos = i * ts + lax.broadcasted_iota(jnp.int32, (ts, 1), 0)
    for g, w in enumerate(POOL_WINDOWS):
        cols = slice(g * POOL_GROUP, (g + 1) * POOL_GROUP)
        win = ext_ref[POOL_HALO:, cols]
        for d in range(1, w):
            win = win + ext_ref[pl.ds(POOL_HALO - d, ts), cols]
        count = jnp.minimum(pos + 1, w).astype(F32)
        pooled = win / count - u[:, cols]
        y = jnp.dot(pooled.astype(BF16), wg_ref[g], preferred_element_type=F32)
        o_ref[:, cols] = x[:, cols] + y * sc_ref[:, cols]


def _pool(h, g, wg, scale, *, batch, seq):
    t, d = h.shape
    ts = min(512, seq)
    ns = seq // ts
    hb = ts // POOL_HALO
    return pl.pallas_call(
        functools.partial(_pool_kernel, ts=ts),
        grid=(batch, ns),
        in_specs=[
            pl.BlockSpec((ts, d), lambda b, i: (b * ns + i, 0)),
            pl.BlockSpec((POOL_HALO, d), lambda b, i: (jnp.maximum((b * ns + i) * hb - 1, 0), 0)),
            pl.BlockSpec((1, d), lambda b, i: (0, 0)),
            pl.BlockSpec(wg.shape, lambda b, i: (0, 0, 0)),
            pl.BlockSpec((1, d), lambda b, i: (0, 0)),
        ],
        out_specs=pl.BlockSpec((ts, d), lambda b, i: (b * ns + i, 0)),
        out_shape=jax.ShapeDtypeStruct((t, d), F32),
        scratch_shapes=[pltpu.VMEM((ts + POOL_HALO, d), F32)],
        compiler_params=_cparams(("parallel", "arbitrary")),
        name="pool",
    )(h, h, g, wg, scale)


def _masked_heads(x, n_pairs, rows):
    xf = x.astype(F32)
    lane = lax.broadcasted_iota(jnp.int32, (rows, LANES), 1)
    out = []
    for p in range(n_pairs):
        xp = xf[:, p * LANES:(p + 1) * LANES]
        out.append(jnp.where(lane < IDX_DIM, xp, 0.0).astype(BF16))
        out.append(jnp.where(lane >= IDX_DIM, xp, 0.0).astype(BF16))
    return out


def _sum_sublane_groups(m, rows):
    parts = [m[r:r + 8] for r in range(0, rows, 8)]
    while len(parts) > 1:
        parts = [parts[a] + parts[a + 1] for a in range(0, len(parts), 2)]
    return parts[0]


def _dsa_kernel(q_ref, k_ref, v_ref, iq_ref, ik_ref, iw_ref, o_ref, sc_ref, scb_ref, vt_ref, qg_ref, m_ref, acc_ref,
                s0_ref, s1_ref, *, tq, kb, seq, topk, idx_bits, idx_scale):
    i = pl.program_id(1)
    rep = C_HEADS // C_KV_HEADS

    @pl.when(i == 0)
    def _():
        for g in range(C_KV_HEADS):
            _store_vt(vt_ref.at[g], v_ref, slice(g * C_HEAD_DIM, (g + 1) * C_HEAD_DIM), seq, kb)

    nblk = ((i + 1) * tq) // kb
    iw = iw_ref[0] * idx_scale
    iqh = _masked_heads(iq_ref[...], IDX_HEADS // 2, tq)
    qchunk = (i * tq + lax.broadcasted_iota(jnp.int32, (kb, tq), 1)) >> CHUNK_SHIFT
    krow = lax.broadcasted_iota(jnp.int32, (kb, tq), 0)

    def score_block(r):
        ks = pl.multiple_of(r * kb, kb)
        ikb = ik_ref[pl.ds(ks, kb), :]
        acc = jnp.zeros((kb, tq), F32)
        for h in range(IDX_HEADS):
            acc = acc + iw[h:h + 1, :] * jnp.maximum(_dot_nt(ikb, iqh[h]), 0.0)
        acc = jnp.where(((ks + krow) >> CHUNK_SHIFT) <= qchunk, acc, -jnp.inf)
        sc_ref[pl.ds(ks, kb), :] = acc
        scb_ref[pl.ds(ks, kb), :] = acc.astype(BF16)

    def score_pair(t, c):
        score_block(2 * t)
        score_block(2 * t + 1)
        return c

    lax.fori_loop(0, nblk // 2, score_pair, 0)

    @pl.when(nblk % 2 == 1)
    def _():
        score_block(nblk - 1)

    n_acc = 4

    def count(pred):
        def body(r, accs):
            ks = pl.multiple_of(r * kb, kb)
            p = pred(sc_ref[pl.ds(ks, kb), :], ks)
            accs = list(accs)
            for g8 in range(kb // 8):
                a = accs[g8 % n_acc]
                accs[g8 % n_acc] = jnp.where(p[g8 * 8:(g8 + 1) * 8], a + 1.0, a)
            return tuple(accs)
        accs = lax.fori_loop(0, nblk, body, (jnp.zeros((8, tq), F32),) * n_acc)
        return jnp.sum(sum(accs), axis=0, keepdims=True)

    int_min = jnp.int32(-2 ** 31)

    def key_to_f32(key_u):
        ks_ = key_u ^ int_min
        bits = jnp.where(ks_ >= 0, ks_, ks_ ^ jnp.int32(0x7FFFFFFF))
        return lax.bitcast_convert_type(bits, F32)

    one_bf = jnp.ones((), BF16)

    def count_coarse(cand_bf):
        def body(r, accs):
            blk = scb_ref[pl.ds(pl.multiple_of(r * kb, kb), kb), :]
            accs = list(accs)
            for g16 in range(kb // 16):
                a = accs[g16 % n_acc]
                accs[g16 % n_acc] = jnp.where(blk[g16 * 16:(g16 + 1) * 16] >= cand_bf, a + one_bf, a)
            return tuple(accs)
        accs = lax.fori_loop(0, nblk, body, (jnp.zeros((16, tq), BF16),) * n_acc)
        return jnp.sum(sum(a.astype(F32) for a in accs), axis=0, keepdims=True)

    def coarse_body(b, t_u):
        cand_u = t_u | lax.shift_left(jnp.int32(1), 31 - b)
        cnt = count_coarse(key_to_f32(cand_u).astype(BF16))
        return jnp.where(cnt >= topk, cand_u, t_u)

    p_u = lax.fori_loop(0, 16, coarse_body, jnp.zeros((1, tq), jnp.int32))

    lo_u = p_u - jnp.int32(0x10000)

    def fine_body(b, off):
        cand_off = off | lax.shift_left(jnp.int32(1), 16 - b)
        cand = key_to_f32(lo_u + cand_off)
        cnt = count(lambda blk, ks: blk >= cand)
        return jnp.where(cnt >= topk, cand_off, off)

    t_u = lo_u + lax.fori_loop(0, 17, fine_body, jnp.zeros((1, tq), jnp.int32))
    n_adm = (((i * tq + lax.broadcasted_iota(jnp.int32, (1, tq), 1)) >> CHUNK_SHIFT) + 1) << CHUNK_SHIFT
    thr = jnp.where(n_adm < topk, -jnp.inf, key_to_f32(t_u))
    n_ge = count(lambda blk, ks: blk >= thr)

    def tie_cut():
        need = topk - count(lambda blk, ks: blk > thr)

        def idx_body(b, j_u):
            cand = j_u | lax.shift_left(jnp.int32(1), idx_bits - 1 - b)
            cnt = count(lambda blk, ks: (blk == thr) & ((ks + krow) < cand))
            return jnp.where(cnt < need, cand, j_u)

        return lax.fori_loop(0, idx_bits, idx_body, jnp.zeros((1, tq), jnp.int32))

    def no_cut():
        return jnp.full((1, tq), 2 ** idx_bits - 1, jnp.int32)

    jcut = lax.cond(jnp.max(n_ge) > topk, tie_cut, no_cut)
    jcut = jnp.where(thr == -jnp.inf, -1, jcut)

    m_ref[...] = jnp.full(m_ref.shape, NEG, F32)
    acc_ref[...] = jnp.zeros_like(acc_ref)
    for hh in range(C_HEADS):
        qg_ref[hh // rep, (hh % rep) * tq:(hh % rep + 1) * tq, :] = q_ref[:, hh * C_HEAD_DIM:(hh + 1) * C_HEAD_DIM]

    def scores(r, buf):
        ks = pl.multiple_of(r * kb, kb)

        def part(g):
            buf[g] = _dot_nt(k_ref[pl.ds(ks, kb), g * C_HEAD_DIM:(g + 1) * C_HEAD_DIM], qg_ref[g])

        return [functools.partial(part, g) for g in range(C_KV_HEADS)]

    def consume(r, buf, is_last):
        ks = pl.multiple_of(r * kb, kb)
        sc = sc_ref[pl.ds(ks, kb), :]
        sel = (sc > thr) | ((sc == thr) & ((ks + krow) <= jcut))
        bias = jnp.where(sel, 0.0, NEG)
        bias = jnp.concatenate([bias] * rep, axis=1)

        def part(g):
            _softmax_step_t(buf[g] + bias, vt_ref[g, :, pl.ds(ks, kb)], m_ref.at[g], acc_ref.at[g])

        return [functools.partial(part, g) for g in range(C_KV_HEADS)]

    def finish():
        for g in range(C_KV_HEADS):
            a = acc_ref[g]
            ot = a[0:LANES] / a[LANES:LANES + 1]
            for e in range(rep):
                hh = g * rep + e
                o_ref[:, hh * C_HEAD_DIM:(hh + 1) * C_HEAD_DIM] = ot[:, e * tq:(e + 1) * tq].T.astype(BF16)

    _pipelined_tiles(nblk, scores, consume, finish, s0_ref, s1_ref)


def _dsa(proj, iw_t, *, batch, seq, topk):
    t = batch * seq
    tq = min(256, seq)
    nq = seq // tq
    kvw = C_KV_HEADS * C_HEAD_DIM
    iqw = IDX_HEADS * IDX_DIM
    return pl.pallas_call(
        functools.partial(_dsa_kernel, tq=tq, kb=tq, seq=seq, topk=float(topk), idx_bits=(seq - 1).bit_length(),
                          idx_scale=IDX_HEADS ** -0.5 * IDX_DIM ** -0.5),
        grid=(batch, nq),
        in_specs=[
            pl.BlockSpec((tq, C_HEADS * C_HEAD_DIM), lambda b, i: (b * nq + i, 0)),
            pl.BlockSpec((seq, kvw), lambda b, i: (b, C_K0 // kvw)),
            pl.BlockSpec((seq, kvw), lambda b, i: (b, C_V0 // kvw)),
            pl.BlockSpec((tq, iqw), lambda b, i: (b * nq + i, C_IQ0 // iqw)),
            pl.BlockSpec((seq, LANES), lambda b, i: (b, C_IK0 // LANES)),
            pl.BlockSpec((1, 8, tq), lambda b, i: (b, 0, i)),
        ],
        out_specs=pl.BlockSpec((tq, C_HEADS * C_HEAD_DIM), lambda b, i: (b * nq + i, 0)),
        out_shape=jax.ShapeDtypeStruct((t, C_HEADS * C_HEAD_DIM), BF16),
        scratch_shapes=[pltpu.VMEM((seq, tq), F32), pltpu.VMEM((seq, tq), BF16),
                        pltpu.VMEM((C_KV_HEADS, V_ROWS, seq), BF16),
                        pltpu.VMEM((C_KV_HEADS, C_HEADS // C_KV_HEADS * tq, C_HEAD_DIM), BF16),
                        pltpu.VMEM((C_KV_HEADS, 1, C_HEADS // C_KV_HEADS * tq), F32),
                        pltpu.VMEM((C_KV_HEADS, V_ROWS, C_HEADS // C_KV_HEADS * tq), F32)]
                       + [pltpu.VMEM((C_KV_HEADS, tq, C_HEADS // C_KV_HEADS * tq), F32)] * 2,
        compiler_params=_cparams(("arbitrary", "arbitrary")),
        name="dsa",
    )(proj, proj, proj, proj, proj, iw_t)


def _pad_c_weight(w):
    iw = w[:, 2112:2120]
    ik = w[:, 2048:2112]
    pad = jnp.zeros((w.shape[0], C_COLS - C_IW0 - IDX_HEADS), w.dtype)
    return jnp.concatenate([w[:, :2048], ik, ik, iw, pad], axis=1).astype(BF16)


def kernel(x, norm1_g, norm2_g, a_w_in, a_q_norm_g, a_k_norm_g, a_lambda_q1, a_lambda_k1, a_lambda_q2,
           a_lambda_k2, a_subln_g, a_w_out, b_w_group, b_scale, c_w_in, c_q_norm_g, c_k_norm_g, c_w_out,
           mlp_w1, mlp_w2):
    batch, seq, d = x.shape
    depth = norm1_g.shape[0]
    h = x.reshape(batch * seq, d)
    row = lambda v: v.reshape(1, -1).astype(F32)
    for i in range(depth):
        m, j = i % N_MIXERS, i // N_MIXERS
        g1 = row(norm1_g[i])
        if m == 0:
            lambda_init = 0.8 - 0.6 * math.exp(-0.3 * i)
            gain = jnp.concatenate([jnp.tile(a_q_norm_g[j], 2 * A_HEADS) * (A_HEAD_DIM ** -0.5 * LOG2E),
                                    jnp.tile(a_k_norm_g[j], 2 * A_HEADS),
                                    jnp.ones((A_HEADS * A_V_DIM,), F32)]).reshape(1, -1)
            qkv = _proj(h, g1, a_w_in[j].astype(BF16), gain, n_norm_cols=2 * D_MODEL, seg=A_HEAD_DIM, aux=False)
            o = _attn_a(qkv, row(a_lambda_q1[j]), row(a_lambda_k1[j]), row(a_lambda_q2[j]), row(a_lambda_k2[j]),
                        a_subln_g[j].reshape(-1, 1).astype(F32), batch=batch, seq=seq, lambda_init=lambda_init)
            h = _outproj(o, a_w_out[j].astype(BF16), h)
        elif m == 1:
            h = _pool(h, g1, b_w_group[j].astype(BF16), row(b_scale[j]), batch=batch, seq=seq)
        else:
            gain = jnp.concatenate([jnp.tile(c_q_norm_g[j], C_HEADS) * (C_HEAD_DIM ** -0.5 * LOG2E),
                                    jnp.tile(c_k_norm_g[j], C_KV_HEADS),
                                    jnp.ones((C_COLS - C_V0,), F32)]).reshape(1, -1)
            proj, aux = _proj(h, g1, _pad_c_weight(c_w_in[j]), gain, n_norm_cols=C_V0, seg=C_HEAD_DIM, aux=True)
            iw_t = aux[:, :IDX_HEADS].reshape(batch, seq, IDX_HEADS).transpose(0, 2, 1)
            o = _dsa(proj, iw_t, batch=batch, seq=seq, topk=min(TOPK_MAX, seq // 4))
            h = _outproj(o, c_w_out[j].astype(BF16), h)
        h = _mlp(h, row(norm2_g[i]), mlp_w1, mlp_w2, i)
    return h.reshape(batch, seq, d)
```

```python
import functools
import math

import jax
import jax.numpy as jnp
from jax import lax
from jax.experimental import pallas as pl
from jax.experimental.pallas import tpu as pltpu

F32 = jnp.float32
BF16 = jnp.bfloat16

D_MODEL = 1024
N_MIXERS = 3
CHUNK = 64
CHUNK_SHIFT = 6
EPS = 1e-6
LANES = 128
MXU_DIM = 256

A_HEADS = 8
A_HEAD_DIM = 64
A_V_DIM = 128
POOL_WINDOWS = (2, 4, 8, 16)
POOL_GROUP = 256
POOL_HALO = 16
C_HEADS = 8
C_HEAD_DIM = 128
C_KV_HEADS = 2
IDX_HEADS = 8
IDX_DIM = 64
TOPK_MAX = 256

C_Q0, C_K0, C_V0, C_IQ0, C_IK0, C_IW0, C_COLS = 0, 1024, 1280, 1536, 2048, 2176, 2304

NEG = -1e30
LOG2E = math.log2(math.e)
V_ROWS = 128 + 16
VMEM_LIMIT = 56 * 1024 * 1024

PROJ_ROWS = 512
DENSE_ROWS = 1024
MLP_FF_COLS = 1024
ATTN_A_TILE = 512
POOL_ROWS = 512
DSA_TILE = 256


def _cparams(sem):
    return pltpu.CompilerParams(dimension_semantics=sem, vmem_limit_bytes=VMEM_LIMIT)


def _rms_rows(x, g):
    ms = jnp.mean(x * x, axis=-1, keepdims=True)
    return (x * lax.rsqrt(ms + EPS)) * g


def _dot_nt(a, b):
    return lax.dot_general(a, b, (((1,), (1,)), ((), ())), preferred_element_type=F32)


def _proj_kernel(h_ref, g_ref, w_ref, gain_ref, seg_ref, o_ref, *aux_ref, n_norm_cols, seg):
    u = _rms_rows(h_ref[...], g_ref[...]).astype(BF16)
    n = w_ref.shape[1]
    for c0 in range(0, n, 2 * MXU_DIM):
        cw = min(2 * MXU_DIM, n - c0)
        acc = jnp.dot(u, w_ref[:, c0:c0 + cw], preferred_element_type=F32)
        for s0 in range(0, cw, MXU_DIM):
            cols = slice(c0 + s0, c0 + s0 + MXU_DIM)
            a = acc[:, s0:s0 + MXU_DIM]
            if c0 + s0 < n_norm_cols:
                sq = a * a
                hi = sq.astype(BF16)
                lo = (sq - hi.astype(F32)).astype(BF16)
                ssum = (jnp.dot(hi, seg_ref[...], preferred_element_type=F32)
                        + jnp.dot(lo, seg_ref[...], preferred_element_type=F32))
                a = (a * lax.rsqrt(ssum * (1.0 / seg) + EPS)) * gain_ref[:, cols]
            o_ref[:, cols] = a.astype(BF16)
        if aux_ref and c0 + cw == n:
            aux_ref[0][...] = acc[:, cw - LANES:]


def _proj(h, g, w, gain, *, n_norm_cols, seg, aux):
    t, d = h.shape
    n = w.shape[1]
    tm = min(PROJ_ROWS, t)
    assert n % MXU_DIM == 0 and n_norm_cols % MXU_DIM == 0 and t % tm == 0
    r = lax.broadcasted_iota(jnp.int32, (MXU_DIM, MXU_DIM), 0) // seg
    c = lax.broadcasted_iota(jnp.int32, (MXU_DIM, MXU_DIM), 1) // seg
    segm = (r == c).astype(BF16)
    out_shape = [jax.ShapeDtypeStruct((t, n), BF16)]
    out_specs = [pl.BlockSpec((tm, n), lambda i: (i, 0))]
    if aux:
        out_shape.append(jax.ShapeDtypeStruct((t, LANES), F32))
        out_specs.append(pl.BlockSpec((tm, LANES), lambda i: (i, 0)))
    res = pl.pallas_call(
        functools.partial(_proj_kernel, n_norm_cols=n_norm_cols, seg=seg),
        grid=(t // tm,),
        in_specs=[
            pl.BlockSpec((tm, d), lambda i: (i, 0)),
            pl.BlockSpec((1, d), lambda i: (0, 0)),
            pl.BlockSpec((d, n), lambda i: (0, 0)),
            pl.BlockSpec((1, n), lambda i: (0, 0)),
            pl.BlockSpec((MXU_DIM, MXU_DIM), lambda i: (0, 0)),
        ],
        out_specs=out_specs,
        out_shape=out_shape,
        compiler_params=_cparams(("parallel",)),
        name="proj",
    )(h, g, w, gain, segm)
    return res if aux else res[0]


def _mlp_kernel(h_ref, g_ref, w1_ref, w2_ref, *rest):
    *mix, o_ref, u_ref = rest
    j = pl.program_id(1)

    @pl.when(j == 0)
    def _():
        hn = h_ref[...]
        if mix:
            hn = hn + jnp.dot(mix[0][...], mix[1][...], preferred_element_type=F32)
        o_ref[...] = hn
        u_ref[...] = _rms_rows(hn, g_ref[...]).astype(BF16)

    hid = jnp.dot(u_ref[...], w1_ref[...].astype(BF16), preferred_element_type=F32)
    hid = jnp.square(jnp.maximum(hid, 0.0)).astype(BF16)
    o_ref[...] += jnp.dot(hid, w2_ref[...].astype(BF16), preferred_element_type=F32)


def _mlp(h, g, w1, w2, layer, mixed=None, w_out=None):
    t, d = h.shape
    f = w1.shape[2]
    tm = min(DENSE_ROWS, t)
    tf = min(MLP_FF_COLS, f)
    operands = [h, g, w1, w2]
    in_specs = [
        pl.BlockSpec((tm, d), lambda i, j: (i, 0)),
        pl.BlockSpec((1, d), lambda i, j: (0, 0)),
        pl.BlockSpec((None, d, tf), lambda i, j: (layer, 0, j)),
        pl.BlockSpec((None, tf, d), lambda i, j: (layer, j, 0)),
    ]
    if mixed is not None:
        operands += [mixed, w_out]
        in_specs += [pl.BlockSpec((tm, mixed.shape[1]), lambda i, j: (i, 0)),
                     pl.BlockSpec(w_out.shape, lambda i, j: (0, 0))]
    return pl.pallas_call(
        _mlp_kernel,
        grid=(t // tm, f // tf),
        in_specs=in_specs,
        out_specs=pl.BlockSpec((tm, d), lambda i, j: (i, 0)),
        out_shape=jax.ShapeDtypeStruct((t, d), F32),
        scratch_shapes=[pltpu.VMEM((tm, d), BF16)],
        compiler_params=_cparams(("parallel", "arbitrary")),
        name="mlp",
    )(*operands)


def _softmax_step_t(st, vt1, m_ref, acc_ref):
    m_prev = m_ref[...]
    m_next = jnp.maximum(m_prev, jnp.max(st, axis=0, keepdims=True))
    p = jnp.exp2(st - m_next).astype(BF16)
    alpha = jnp.exp2(m_prev - m_next)
    m_ref[...] = m_next
    acc_ref[...] = alpha * acc_ref[...] + jnp.dot(vt1, p, preferred_element_type=F32)


def _pipelined_tiles(n, scores, consume, finish, buf_a, buf_b):
    n_pairs = (n - 1) // 2

    def run(parts):
        for part in parts:
            part()

    def both(j_scores, buf_scores, j_consume, buf_consume):
        for s_part, c_part in zip(scores(j_scores, buf_scores), consume(j_consume, buf_consume, False)):
            s_part()
            c_part()

    run(scores(0, buf_a))

    def pair(t, c):
        both(2 * t + 1, buf_b, 2 * t, buf_a)
        both(2 * t + 2, buf_a, 2 * t + 1, buf_b)
        return c

    lax.fori_loop(0, n_pairs, pair, 0)

    @pl.when(n - 2 * n_pairs == 1)
    def _():
        run(consume(n - 1, buf_a, True))
        finish()

    @pl.when(n - 2 * n_pairs == 2)
    def _():
        both(n - 1, buf_b, n - 2, buf_a)
        run(consume(n - 1, buf_b, True))
        finish()


def _store_vt(vt_ref, v_ref, cols, seq, blk):
    for c in range(seq // blk):
        rows = slice(c * blk, (c + 1) * blk)
        vt_ref[0:LANES, rows] = v_ref[rows, cols].astype(F32).T.astype(BF16)
    vt_ref[LANES:, :] = jnp.ones((V_ROWS - LANES, seq), BF16)


A_HEADS_PER_STEP = 4


def _attn_a_kernel(q_ref, k_ref, v_ref, lq1_ref, lk1_ref, lq2_ref, lk2_ref, subg_ref, o_ref,
                   vt_ref, qm_ref, m_ref, acc_ref, s0_ref, s1_ref, *, tq, tk, seq, lambda_init):
    i = pl.program_id(2)
    heads = range(A_HEADS_PER_STEP)
    hcols = lambda hh: slice(hh * LANES, (hh + 1) * LANES)

    @pl.when(i == 0)
    def _():
        for hh in heads:
            _store_vt(vt_ref.at[hh], v_ref, hcols(hh), seq, tk)

    lane = lax.broadcasted_iota(jnp.int32, (tq, LANES), 1)
    for hh in heads:
        q = q_ref[:, hcols(hh)].astype(F32)
        qm_ref[hh, 0:tq, :] = jnp.where(lane < A_HEAD_DIM, q, 0.0).astype(BF16)
        qm_ref[hh, tq:, :] = jnp.where(lane >= A_HEAD_DIM, q, 0.0).astype(BF16)
    m_ref[...] = jnp.full(m_ref.shape, NEG, F32)
    acc_ref[...] = jnp.zeros_like(acc_ref)

    parts = [(hh, slice(mp * tq, (mp + 1) * tq)) for hh in heads for mp in range(2)]

    def scores(j, buf):
        ks = pl.multiple_of(j * tk, tk)

        def part(hh, lanes):
            buf[hh, :, lanes] = _dot_nt(k_ref[pl.ds(ks, tk), hcols(hh)], qm_ref[hh, lanes, :])

        return [functools.partial(part, hh, lanes) for hh, lanes in parts]

    def consume(j, buf, masked):
        ks = pl.multiple_of(j * tk, tk)
        if masked:
            kpos = ks + lax.broadcasted_iota(jnp.int32, (tk, tq), 0)
            qpos = i * tq + lax.broadcasted_iota(jnp.int32, (tk, tq), 1)
            ok = (kpos >> CHUNK_SHIFT) <= (qpos >> CHUNK_SHIFT)

        def part(hh, lanes):
            st = buf[hh, :, lanes]
            if masked:
                st = jnp.where(ok, st, NEG)
            _softmax_step_t(st, vt_ref[hh, :, pl.ds(ks, tk)], m_ref.at[hh, :, lanes], acc_ref.at[hh, :, lanes])

        return [functools.partial(part, hh, lanes) for hh, lanes in parts]

    def finish():
        lam = (jnp.exp(jnp.sum(lq1_ref[...] * lk1_ref[...], axis=1, keepdims=True))
               - jnp.exp(jnp.sum(lq2_ref[...] * lk2_ref[...], axis=1, keepdims=True)) + lambda_init)
        for hh in heads:
            acc = acc_ref[hh]
            ot = acc[0:LANES] / acc[LANES:LANES + 1]
            od = ot[:, :tq] - lam * ot[:, tq:]
            ms = jnp.mean(od * od, axis=0, keepdims=True)
            y = (od * lax.rsqrt(ms + EPS)) * subg_ref[...] * (1.0 - lambda_init)
            o_ref[:, hcols(hh)] = y.T.astype(BF16)

    _pipelined_tiles(((i + 1) * tq + tk - 1) // tk, scores, consume, finish, s0_ref, s1_ref)


def _attn_a(qkv, lq1, lk1, lq2, lk2, subg, *, batch, seq, lambda_init):
    t = batch * seq
    tq = tk = min(ATTN_A_TILE, seq)
    nq = seq // tq
    hp = A_HEADS_PER_STEP
    ngrp = A_HEADS // hp
    vec = lambda n: pl.BlockSpec((1, n), lambda b, h, i: (0, 0))
    return pl.pallas_call(
        functools.partial(_attn_a_kernel, tq=tq, tk=tk, seq=seq, lambda_init=lambda_init),
        grid=(batch, ngrp, nq),
        in_specs=[
            pl.BlockSpec((tq, hp * LANES), lambda b, h, i: (b * nq + i, h)),
            pl.BlockSpec((seq, hp * LANES), lambda b, h, i: (b, ngrp + h)),
            pl.BlockSpec((seq, hp * LANES), lambda b, h, i: (b, 2 * ngrp + h)),
            vec(A_HEAD_DIM), vec(A_HEAD_DIM), vec(A_HEAD_DIM), vec(A_HEAD_DIM),
            pl.BlockSpec((A_V_DIM, 1), lambda b, h, i: (0, 0)),
        ],
        out_specs=pl.BlockSpec((tq, hp * LANES), lambda b, h, i: (b * nq + i, h)),
        out_shape=jax.ShapeDtypeStruct((t, A_HEADS * A_V_DIM), BF16),
        scratch_shapes=[pltpu.VMEM((hp, V_ROWS, seq), BF16), pltpu.VMEM((hp, 2 * tq, LANES), BF16),
                        pltpu.VMEM((hp, 1, 2 * tq), F32),
                        pltpu.VMEM((hp, V_ROWS, 2 * tq), F32)] + [pltpu.VMEM((hp, tk, 2 * tq), F32)] * 2,
        compiler_params=_cparams(("arbitrary", "arbitrary", "arbitrary")),
        name="attn_a",
    )(qkv, qkv, qkv, lq1, lk1, lq2, lk2, subg)


def _pool_kernel(h_ref, halo_ref, g_ref, wg_ref, sc_ref, o_ref, ext_ref, *, ts):
    i = pl.program_id(1)
    x = h_ref[...]
    u = _rms_rows(x, g_ref[...])
    uh = _rms_rows(halo_ref[...], g_ref[...])
    ext_ref[0:POOL_HALO, :] = jnp.where(i > 0, uh, 0.0)
    ext_ref[POOL_HALO:, :] = u
    pos = i * ts + lax.broadcasted_iota(jnp.int32, (ts, 1), 0)
    for g, w in enumerate(POOL_WINDOWS):
        cols = slice(g * POOL_GROUP, (g + 1) * POOL_GROUP)
        win = ext_ref[POOL_HALO:, cols]
        for d in range(1, w):
            win = win + ext_ref[pl.ds(POOL_HALO - d, ts), cols]
        count = jnp.minimum(pos + 1, w).astype(F32)
        pooled = win / count - u[:, cols]
        y = jnp.dot(pooled.astype(BF16), wg_ref[g], preferred_element_type=F32)
        o_ref[:, cols] = x[:, cols] + y * sc_ref[:, cols]


def _pool(h, g, wg, scale, *, batch, seq):
    t, d = h.shape
    ts = min(POOL_ROWS, seq)
    ns = seq // ts
    hb = ts // POOL_HALO
    return pl.pallas_call(
        functools.partial(_pool_kernel, ts=ts),
        grid=(batch, ns),
        in_specs=[
            pl.BlockSpec((ts, d), lambda b, i: (b * ns + i, 0)),
            pl.BlockSpec((POOL_HALO, d), lambda b, i: (jnp.maximum((b * ns + i) * hb - 1, 0), 0)),
            pl.BlockSpec((1, d), lambda b, i: (0, 0)),
            pl.BlockSpec(wg.shape, lambda b, i: (0, 0, 0)),
            pl.BlockSpec((1, d), lambda b, i: (0, 0)),
        ],
        out_specs=pl.BlockSpec((ts, d), lambda b, i: (b * ns + i, 0)),
        out_shape=jax.ShapeDtypeStruct((t, d), F32),
        scratch_shapes=[pltpu.VMEM((ts + POOL_HALO, d), F32)],
        compiler_params=_cparams(("parallel", "arbitrary")),
        name="pool",
    )(h, h, g, wg, scale)


def _masked_heads(x, n_pairs, rows):
    xf = x.astype(F32)
    lane = lax.broadcasted_iota(jnp.int32, (rows, LANES), 1)
    out = []
    for p in range(n_pairs):
        xp = xf[:, p * LANES:(p + 1) * LANES]
        out.append(jnp.where(lane < IDX_DIM, xp, 0.0).astype(BF16))
        out.append(jnp.where(lane >= IDX_DIM, xp, 0.0).astype(BF16))
    return out


def _sum_sublane_groups(m, rows):
    parts = [m[r:r + 8] for r in range(0, rows, 8)]
    while len(parts) > 1:
        parts = [parts[a] + parts[a + 1] for a in range(0, len(parts), 2)]
    return parts[0]


def _dsa_kernel(q_ref, k_ref, v_ref, iq_ref, ik_ref, iw_ref, o_ref, sc_ref, scb_ref, vt_ref, qg_ref, m_ref, acc_ref,
                s0_ref, s1_ref, *, tq, kb, seq, topk, idx_bits, idx_scale):
    i = pl.program_id(1)
    rep = C_HEADS // C_KV_HEADS

    @pl.when(i == 0)
    def _():
        for g in range(C_KV_HEADS):
            _store_vt(vt_ref.at[g], v_ref, slice(g * C_HEAD_DIM, (g + 1) * C_HEAD_DIM), seq, kb)

    nblk = ((i + 1) * tq) // kb
    iw = iw_ref[0] * idx_scale
    iqh = _masked_heads(iq_ref[...], IDX_HEADS // 2, tq)
    qchunk = (i * tq + lax.broadcasted_iota(jnp.int32, (kb, tq), 1)) >> CHUNK_SHIFT
    krow = lax.broadcasted_iota(jnp.int32, (kb, tq), 0)

    def score_block(r):
        ks = pl.multiple_of(r * kb, kb)
        ikb = ik_ref[pl.ds(ks, kb), :]
        acc = jnp.zeros((kb, tq), F32)
        for h in range(IDX_HEADS):
            acc = acc + iw[h:h + 1, :] * jnp.maximum(_dot_nt(ikb, iqh[h]), 0.0)
        acc = jnp.where(((ks + krow) >> CHUNK_SHIFT) <= qchunk, acc, -jnp.inf)
        sc_ref[pl.ds(ks, kb), :] = acc
        scb_ref[pl.ds(ks, kb), :] = acc.astype(BF16)

    def score_pair(t, c):
        score_block(2 * t)
        score_block(2 * t + 1)
        return c

    lax.fori_loop(0, nblk // 2, score_pair, 0)

    @pl.when(nblk % 2 == 1)
    def _():
        score_block(nblk - 1)

    n_acc = 4

    def count(pred):
        def body(r, accs):
            ks = pl.multiple_of(r * kb, kb)
            p = pred(sc_ref[pl.ds(ks, kb), :], ks)
            accs = list(accs)
            for g8 in range(kb // 8):
                a = accs[g8 % n_acc]
                accs[g8 % n_acc] = jnp.where(p[g8 * 8:(g8 + 1) * 8], a + 1.0, a)
            return tuple(accs)
        accs = lax.fori_loop(0, nblk, body, (jnp.zeros((8, tq), F32),) * n_acc)
        return jnp.sum(sum(accs), axis=0, keepdims=True)

    int_min = jnp.int32(-2 ** 31)

    def key_to_f32(key_u):
        ks_ = key_u ^ int_min
        bits = jnp.where(ks_ >= 0, ks_, ks_ ^ jnp.int32(0x7FFFFFFF))
        return lax.bitcast_convert_type(bits, F32)

    one_bf = jnp.ones((), BF16)

    def count_coarse(cand_bf):
        def body(r, accs):
            blk = scb_ref[pl.ds(pl.multiple_of(r * kb, kb), kb), :]
            accs = list(accs)
            for g16 in range(kb // 16):
                a = accs[g16 % n_acc]
                accs[g16 % n_acc] = jnp.where(blk[g16 * 16:(g16 + 1) * 16] >= cand_bf, a + one_bf, a)
            return tuple(accs)
        accs = lax.fori_loop(0, nblk, body, (jnp.zeros((16, tq), BF16),) * n_acc)
        return jnp.sum(sum(a.astype(F32) for a in accs), axis=0, keepdims=True)

    def coarse_body(b, t_u):
        cand_u = t_u | lax.shift_left(jnp.int32(1), 31 - b)
        cnt = count_coarse(key_to_f32(cand_u).astype(BF16))
        return jnp.where(cnt >= topk, cand_u, t_u)

    p_u = lax.fori_loop(0, 16, coarse_body, jnp.zeros((1, tq), jnp.int32))

    lo_u = p_u - jnp.int32(0x10000)

    def fine_body(b, off):
        cand_off = off | lax.shift_left(jnp.int32(1), 16 - b)
        cand = key_to_f32(lo_u + cand_off)
        cnt = count(lambda blk, ks: blk >= cand)
        return jnp.where(cnt >= topk, cand_off, off)

    t_u = lo_u + lax.fori_loop(0, 17, fine_body, jnp.zeros((1, tq), jnp.int32))
    n_adm = (((i * tq + lax.broadcasted_iota(jnp.int32, (1, tq), 1)) >> CHUNK_SHIFT) + 1) << CHUNK_SHIFT
    thr = jnp.where(n_adm < topk, -jnp.inf, key_to_f32(t_u))
    n_ge = count(lambda blk, ks: blk >= thr)

    def tie_cut():
        need = topk - count(lambda blk, ks: blk > thr)

        def idx_body(b, j_u):
            cand = j_u | lax.shift_left(jnp.int32(1), idx_bits - 1 - b)
            cnt = count(lambda blk, ks: (blk == thr) & ((ks + krow) < cand))
            return jnp.where(cnt < need, cand, j_u)

        return lax.fori_loop(0, idx_bits, idx_body, jnp.zeros((1, tq), jnp.int32))

    def no_cut():
        return jnp.full((1, tq), 2 ** idx_bits - 1, jnp.int32)

    jcut = lax.cond(jnp.max(n_ge) > topk, tie_cut, no_cut)
    jcut = jnp.where(thr == -jnp.inf, -1, jcut)

    m_ref[...] = jnp.full(m_ref.shape, NEG, F32)
    acc_ref[...] = jnp.zeros_like(acc_ref)
    for hh in range(C_HEADS):
        qg_ref[hh // rep, (hh % rep) * tq:(hh % rep + 1) * tq, :] = q_ref[:, hh * C_HEAD_DIM:(hh + 1) * C_HEAD_DIM]

    def scores(r, buf):
        ks = pl.multiple_of(r * kb, kb)

        def part(g):
            buf[g] = _dot_nt(k_ref[pl.ds(ks, kb), g * C_HEAD_DIM:(g + 1) * C_HEAD_DIM], qg_ref[g])

        return [functools.partial(part, g) for g in range(C_KV_HEADS)]

    def consume(r, buf, is_last):
        ks = pl.multiple_of(r * kb, kb)
        sc = sc_ref[pl.ds(ks, kb), :]
        sel = (sc > thr) | ((sc == thr) & ((ks + krow) <= jcut))
        bias = jnp.where(sel, 0.0, NEG)
        bias = jnp.concatenate([bias] * rep, axis=1)

        def part(g):
            _softmax_step_t(buf[g] + bias, vt_ref[g, :, pl.ds(ks, kb)], m_ref.at[g], acc_ref.at[g])

        return [functools.partial(part, g) for g in range(C_KV_HEADS)]

    def finish():
        for g in range(C_KV_HEADS):
            a = acc_ref[g]
            ot = a[0:LANES] / a[LANES:LANES + 1]
            for e in range(rep):
                hh = g * rep + e
                o_ref[:, hh * C_HEAD_DIM:(hh + 1) * C_HEAD_DIM] = ot[:, e * tq:(e + 1) * tq].T.astype(BF16)

    _pipelined_tiles(nblk, scores, consume, finish, s0_ref, s1_ref)


def _dsa(proj, iw_t, *, batch, seq, topk):
    t = batch * seq
    tq = min(DSA_TILE, seq)
    nq = seq // tq
    kvw = C_KV_HEADS * C_HEAD_DIM
    iqw = IDX_HEADS * IDX_DIM
    return pl.pallas_call(
        functools.partial(_dsa_kernel, tq=tq, kb=tq, seq=seq, topk=float(topk), idx_bits=(seq - 1).bit_length(),
                          idx_scale=IDX_HEADS ** -0.5 * IDX_DIM ** -0.5),
        grid=(batch, nq),
        in_specs=[
            pl.BlockSpec((tq, C_HEADS * C_HEAD_DIM), lambda b, i: (b * nq + i, 0)),
            pl.BlockSpec((seq, kvw), lambda b, i: (b, C_K0 // kvw)),
            pl.BlockSpec((seq, kvw), lambda b, i: (b, C_V0 // kvw)),
            pl.BlockSpec((tq, iqw), lambda b, i: (b * nq + i, C_IQ0 // iqw)),
            pl.BlockSpec((seq, LANES), lambda b, i: (b, C_IK0 // LANES)),
            pl.BlockSpec((1, 8, tq), lambda b, i: (b, 0, i)),
        ],
        out_specs=pl.BlockSpec((tq, C_HEADS * C_HEAD_DIM), lambda b, i: (b * nq + i, 0)),
        out_shape=jax.ShapeDtypeStruct((t, C_HEADS * C_HEAD_DIM), BF16),
        scratch_shapes=[pltpu.VMEM((seq, tq), F32), pltpu.VMEM((seq, tq), BF16),
                        pltpu.VMEM((C_KV_HEADS, V_ROWS, seq), BF16),
                        pltpu.VMEM((C_KV_HEADS, C_HEADS // C_KV_HEADS * tq, C_HEAD_DIM), BF16),
                        pltpu.VMEM((C_KV_HEADS, 1, C_HEADS // C_KV_HEADS * tq), F32),
                        pltpu.VMEM((C_KV_HEADS, V_ROWS, C_HEADS // C_KV_HEADS * tq), F32)]
                       + [pltpu.VMEM((C_KV_HEADS, tq, C_HEADS // C_KV_HEADS * tq), F32)] * 2,
        compiler_params=_cparams(("arbitrary", "arbitrary")),
        name="dsa",
    )(proj, proj, proj, proj, proj, iw_t)


def _pad_c_weight(w):
    iw = w[:, 2112:2120]
    ik = w[:, 2048:2112]
    pad = jnp.zeros((w.shape[0], C_COLS - C_IW0 - IDX_HEADS), w.dtype)
    return jnp.concatenate([w[:, :2048], ik, ik, iw, pad], axis=1).astype(BF16)


def kernel(x, norm1_g, norm2_g, a_w_in, a_q_norm_g, a_k_norm_g, a_lambda_q1, a_lambda_k1, a_lambda_q2,
           a_lambda_k2, a_subln_g, a_w_out, b_w_group, b_scale, c_w_in, c_q_norm_g, c_k_norm_g, c_w_out,
           mlp_w1, mlp_w2):
    batch, seq, d = x.shape
    depth = norm1_g.shape[0]
    h = x.reshape(batch * seq, d)
    row = lambda v: v.reshape(1, -1).astype(F32)
    for i in range(depth):
        m, j = i % N_MIXERS, i // N_MIXERS
        g1 = row(norm1_g[i])
        mixed = w_out = None
        if m == 0:
            lambda_init = 0.8 - 0.6 * math.exp(-0.3 * i)
            gain = jnp.concatenate([jnp.tile(a_q_norm_g[j], 2 * A_HEADS) * (A_HEAD_DIM ** -0.5 * LOG2E),
                                    jnp.tile(a_k_norm_g[j], 2 * A_HEADS),
                                    jnp.ones((A_HEADS * A_V_DIM,), F32)]).reshape(1, -1)
            qkv = _proj(h, g1, a_w_in[j].astype(BF16), gain, n_norm_cols=2 * D_MODEL, seg=A_HEAD_DIM, aux=False)
            mixed = _attn_a(qkv, row(a_lambda_q1[j]), row(a_lambda_k1[j]), row(a_lambda_q2[j]), row(a_lambda_k2[j]),
                            a_subln_g[j].reshape(-1, 1).astype(F32), batch=batch, seq=seq, lambda_init=lambda_init)
            w_out = a_w_out[j].astype(BF16)
        elif m == 1:
            h = _pool(h, g1, b_w_group[j].astype(BF16), row(b_scale[j]), batch=batch, seq=seq)
        else:
            gain = jnp.concatenate([jnp.tile(c_q_norm_g[j], C_HEADS) * (C_HEAD_DIM ** -0.5 * LOG2E),
                                    jnp.tile(c_k_norm_g[j], C_KV_HEADS),
                                    jnp.ones((C_COLS - C_V0,), F32)]).reshape(1, -1)
            proj, aux = _proj(h, g1, _pad_c_weight(c_w_in[j]), gain, n_norm_cols=C_V0, seg=C_HEAD_DIM, aux=True)
            iw_t = aux[:, :IDX_HEADS].reshape(batch, seq, IDX_HEADS).transpose(0, 2, 1)
            mixed = _dsa(proj, iw_t, batch=batch, seq=seq, topk=min(TOPK_MAX, seq // 4))
            w_out = c_w_out[j].astype(BF16)
        h = _mlp(h, row(norm2_g[i]), mlp_w1, mlp_w2, i, mixed, w_out)
    return h.reshape(batch, seq, d)
```

```python
import functools
import math

import jax
import jax.numpy as jnp
from jax import lax
from jax.experimental import pallas as pl
from jax.experimental.pallas import tpu as pltpu

F32 = jnp.float32
BF16 = jnp.bfloat16

D_MODEL = 1024
N_MIXERS = 3
CHUNK = 64
CHUNK_SHIFT = 6
EPS = 1e-6
LANES = 128
MXU_DIM = 256

A_HEADS = 8
A_HEAD_DIM = 64
A_V_DIM = 128
POOL_WINDOWS = (2, 4, 8, 16)
POOL_GROUP = 256
POOL_HALO = 16
C_HEADS = 8
C_HEAD_DIM = 128
C_KV_HEADS = 2
IDX_HEADS = 8
IDX_DIM = 64
TOPK_MAX = 256

C_Q0, C_K0, C_V0, C_IQ0, C_IK0, C_IW0, C_COLS = 0, 1024, 1280, 1536, 2048, 2176, 2304

NEG = -1e30
LOG2E = math.log2(math.e)
V_ROWS = 128 + 16
VMEM_LIMIT = 56 * 1024 * 1024

PROJ_ROWS = 1024
DENSE_ROWS = 1024
MLP_FF_COLS = 1024
ATTN_A_TILE = 512
POOL_ROWS = 512
DSA_TILE = 256


def _cparams(sem):
    return pltpu.CompilerParams(dimension_semantics=sem, vmem_limit_bytes=VMEM_LIMIT)


def _rms_rows(x, g):
    ms = jnp.mean(x * x, axis=-1, keepdims=True)
    return (x * lax.rsqrt(ms + EPS)) * g


def _dot_nt(a, b):
    return lax.dot_general(a, b, (((1,), (1,)), ((), ())), preferred_element_type=F32)


def _proj_kernel(h_ref, g_ref, w_ref, gain_ref, seg_ref, o_ref, *aux_ref, n_norm_cols, seg):
    u = _rms_rows(h_ref[...], g_ref[...]).astype(BF16)
    n = w_ref.shape[1]
    for c0 in range(0, n, 2 * MXU_DIM):
        cw = min(2 * MXU_DIM, n - c0)
        acc = jnp.dot(u, w_ref[:, c0:c0 + cw], preferred_element_type=F32)
        for s0 in range(0, cw, MXU_DIM):
            cols = slice(c0 + s0, c0 + s0 + MXU_DIM)
            a = acc[:, s0:s0 + MXU_DIM]
            if c0 + s0 < n_norm_cols:
                sq = a * a
                hi = sq.astype(BF16)
                lo = (sq - hi.astype(F32)).astype(BF16)
                ssum = (jnp.dot(hi, seg_ref[...], preferred_element_type=F32)
                        + jnp.dot(lo, seg_ref[...], preferred_element_type=F32))
                a = (a * lax.rsqrt(ssum * (1.0 / seg) + EPS)) * gain_ref[:, cols]
            o_ref[:, cols] = a.astype(BF16)
        if aux_ref and c0 + cw == n:
            aux_ref[0][...] = acc[:, cw - LANES:]


def _proj(h, g, w, gain, *, n_norm_cols, seg, aux):
    t, d = h.shape
    n = w.shape[1]
    tm = min(PROJ_ROWS, t)
    assert n % MXU_DIM == 0 and n_norm_cols % MXU_DIM == 0 and t % tm == 0
    r = lax.broadcasted_iota(jnp.int32, (MXU_DIM, MXU_DIM), 0) // seg
    c = lax.broadcasted_iota(jnp.int32, (MXU_DIM, MXU_DIM), 1) // seg
    segm = (r == c).astype(BF16)
    out_shape = [jax.ShapeDtypeStruct((t, n), BF16)]
    out_specs = [pl.BlockSpec((tm, n), lambda i: (i, 0))]
    if aux:
        out_shape.append(jax.ShapeDtypeStruct((t, LANES), F32))
        out_specs.append(pl.BlockSpec((tm, LANES), lambda i: (i, 0)))
    res = pl.pallas_call(
        functools.partial(_proj_kernel, n_norm_cols=n_norm_cols, seg=seg),
        grid=(t // tm,),
        in_specs=[
            pl.BlockSpec((tm, d), lambda i: (i, 0)),
            pl.BlockSpec((1, d), lambda i: (0, 0)),
            pl.BlockSpec((d, n), lambda i: (0, 0)),
            pl.BlockSpec((1, n), lambda i: (0, 0)),
            pl.BlockSpec((MXU_DIM, MXU_DIM), lambda i: (0, 0)),
        ],
        out_specs=out_specs,
        out_shape=out_shape,
        compiler_params=_cparams(("parallel",)),
        name="proj",
    )(h, g, w, gain, segm)
    return res if aux else res[0]


def _mlp_kernel(h_ref, g_ref, w1_ref, w2_ref, *rest):
    *mix, o_ref, u_ref = rest
    j = pl.program_id(1)

    @pl.when(j == 0)
    def _():
        hn = h_ref[...]
        if mix:
            hn = hn + jnp.dot(mix[0][...], mix[1][...], preferred_element_type=F32)
        o_ref[...] = hn
        u_ref[...] = _rms_rows(hn, g_ref[...]).astype(BF16)

    hid = jnp.dot(u_ref[...], w1_ref[...].astype(BF16), preferred_element_type=F32)
    hid = jnp.square(jnp.maximum(hid, 0.0)).astype(BF16)
    o_ref[...] += jnp.dot(hid, w2_ref[...].astype(BF16), preferred_element_type=F32)


def _mlp(h, g, w1, w2, layer, mixed=None, w_out=None):
    t, d = h.shape
    f = w1.shape[2]
    tm = min(DENSE_ROWS, t)
    tf = min(MLP_FF_COLS, f)
    operands = [h, g, w1, w2]
    in_specs = [
        pl.BlockSpec((tm, d), lambda i, j: (i, 0)),
        pl.BlockSpec((1, d), lambda i, j: (0, 0)),
        pl.BlockSpec((None, d, tf), lambda i, j: (layer, 0, j)),
        pl.BlockSpec((None, tf, d), lambda i, j: (layer, j, 0)),
    ]
    if mixed is not None:
        operands += [mixed, w_out]
        in_specs += [pl.BlockSpec((tm, mixed.shape[1]), lambda i, j: (i, 0)),
                     pl.BlockSpec(w_out.shape, lambda i, j: (0, 0))]
    return pl.pallas_call(
        _mlp_kernel,
        grid=(t // tm, f // tf),
        in_specs=in_specs,
        out_specs=pl.BlockSpec((tm, d), lambda i, j: (i, 0)),
        out_shape=jax.ShapeDtypeStruct((t, d), F32),
        scratch_shapes=[pltpu.VMEM((tm, d), BF16)],
        compiler_params=_cparams(("parallel", "arbitrary")),
        name="mlp",
    )(*operands)


def _softmax_step_t(st, vt1, m_ref, acc_ref):
    m_prev = m_ref[...]
    m_next = jnp.maximum(m_prev, jnp.max(st, axis=0, keepdims=True))
    p = jnp.exp2(st - m_next).astype(BF16)
    alpha = jnp.exp2(m_prev - m_next)
    m_ref[...] = m_next
    acc_ref[...] = alpha * acc_ref[...] + jnp.dot(vt1, p, preferred_element_type=F32)


def _pipelined_tiles(n, scores, consume, finish, buf_a, buf_b):
    n_pairs = (n - 1) // 2

    def run(parts):
        for part in parts:
            part()

    def both(j_scores, buf_scores, j_consume, buf_consume):
        for s_part, c_part in zip(scores(j_scores, buf_scores), consume(j_consume, buf_consume, False)):
            s_part()
            c_part()

    run(scores(0, buf_a))

    def pair(t, c):
        both(2 * t + 1, buf_b, 2 * t, buf_a)
        both(2 * t + 2, buf_a, 2 * t + 1, buf_b)
        return c

    lax.fori_loop(0, n_pairs, pair, 0)

    @pl.when(n - 2 * n_pairs == 1)
    def _():
        run(consume(n - 1, buf_a, True))
        finish()

    @pl.when(n - 2 * n_pairs == 2)
    def _():
        both(n - 1, buf_b, n - 2, buf_a)
        run(consume(n - 1, buf_b, True))
        finish()


def _store_vt(vt_ref, v_ref, cols, seq, blk):
    for c in range(seq // blk):
        rows = slice(c * blk, (c + 1) * blk)
        vt_ref[0:LANES, rows] = v_ref[rows, cols].astype(F32).T.astype(BF16)
    vt_ref[LANES:, :] = jnp.ones((V_ROWS - LANES, seq), BF16)


A_HEADS_PER_STEP = 4


def _attn_a_kernel(q_ref, k_ref, v_ref, lq1_ref, lk1_ref, lq2_ref, lk2_ref, subg_ref, o_ref,
                   vt_ref, qm_ref, m_ref, acc_ref, s0_ref, s1_ref, *, tq, tk, seq, lambda_init):
    i = pl.program_id(2)
    heads = range(A_HEADS_PER_STEP)
    hcols = lambda hh: slice(hh * LANES, (hh + 1) * LANES)

    @pl.when(i == 0)
    def _():
        for hh in heads:
            _store_vt(vt_ref.at[hh], v_ref, hcols(hh), seq, tk)

    lane = lax.broadcasted_iota(jnp.int32, (tq, LANES), 1)
    for hh in heads:
        q = q_ref[:, hcols(hh)].astype(F32)
        qm_ref[hh, 0:tq, :] = jnp.where(lane < A_HEAD_DIM, q, 0.0).astype(BF16)
        qm_ref[hh, tq:, :] = jnp.where(lane >= A_HEAD_DIM, q, 0.0).astype(BF16)
    m_ref[...] = jnp.full(m_ref.shape, NEG, F32)
    acc_ref[...] = jnp.zeros_like(acc_ref)

    parts = [(hh, slice(mp * tq, (mp + 1) * tq)) for hh in heads for mp in range(2)]

    def scores(j, buf):
        ks = pl.multiple_of(j * tk, tk)

        def part(hh, lanes):
            buf[hh, :, lanes] = _dot_nt(k_ref[pl.ds(ks, tk), hcols(hh)], qm_ref[hh, lanes, :])

        return [functools.partial(part, hh, lanes) for hh, lanes in parts]

    def consume(j, buf, masked):
        ks = pl.multiple_of(j * tk, tk)
        if masked:
            kpos = ks + lax.broadcasted_iota(jnp.int32, (tk, tq), 0)
            qpos = i * tq + lax.broadcasted_iota(jnp.int32, (tk, tq), 1)
            ok = (kpos >> CHUNK_SHIFT) <= (qpos >> CHUNK_SHIFT)

        def part(hh, lanes):
            st = buf[hh, :, lanes]
            if masked:
                st = jnp.where(ok, st, NEG)
            _softmax_step_t(st, vt_ref[hh, :, pl.ds(ks, tk)], m_ref.at[hh, :, lanes], acc_ref.at[hh, :, lanes])

        return [functools.partial(part, hh, lanes) for hh, lanes in parts]

    def finish():
        lam = (jnp.exp(jnp.sum(lq1_ref[...] * lk1_ref[...], axis=1, keepdims=True))
               - jnp.exp(jnp.sum(lq2_ref[...] * lk2_ref[...], axis=1, keepdims=True)) + lambda_init)
        for hh in heads:
            acc = acc_ref[hh]
            ot = acc[0:LANES] / acc[LANES:LANES + 1]
            od = ot[:, :tq] - lam * ot[:, tq:]
            ms = jnp.mean(od * od, axis=0, keepdims=True)
            y = (od * lax.rsqrt(ms + EPS)) * subg_ref[...] * (1.0 - lambda_init)
            o_ref[:, hcols(hh)] = y.T.astype(BF16)

    _pipelined_tiles(((i + 1) * tq + tk - 1) // tk, scores, consume, finish, s0_ref, s1_ref)


def _attn_a(qkv, lq1, lk1, lq2, lk2, subg, *, batch, seq, lambda_init):
    t = batch * seq
    tq = tk = min(ATTN_A_TILE, seq)
    nq = seq // tq
    hp = A_HEADS_PER_STEP
    ngrp = A_HEADS // hp
    vec = lambda n: pl.BlockSpec((1, n), lambda b, h, i: (0, 0))
    return pl.pallas_call(
        functools.partial(_attn_a_kernel, tq=tq, tk=tk, seq=seq, lambda_init=lambda_init),
        grid=(batch, ngrp, nq),
        in_specs=[
            pl.BlockSpec((tq, hp * LANES), lambda b, h, i: (b * nq + i, h)),
            pl.BlockSpec((seq, hp * LANES), lambda b, h, i: (b, ngrp + h)),
            pl.BlockSpec((seq, hp * LANES), lambda b, h, i: (b, 2 * ngrp + h)),
            vec(A_HEAD_DIM), vec(A_HEAD_DIM), vec(A_HEAD_DIM), vec(A_HEAD_DIM),
            pl.BlockSpec((A_V_DIM, 1), lambda b, h, i: (0, 0)),
        ],
        out_specs=pl.BlockSpec((tq, hp * LANES), lambda b, h, i: (b * nq + i, h)),
        out_shape=jax.ShapeDtypeStruct((t, A_HEADS * A_V_DIM), BF16),
        scratch_shapes=[pltpu.VMEM((hp, V_ROWS, seq), BF16), pltpu.VMEM((hp, 2 * tq, LANES), BF16),
                        pltpu.VMEM((hp, 1, 2 * tq), F32),
                        pltpu.VMEM((hp, V_ROWS, 2 * tq), F32)] + [pltpu.VMEM((hp, tk, 2 * tq), F32)] * 2,
        compiler_params=_cparams(("arbitrary", "arbitrary", "arbitrary")),
        name="attn_a",
    )(qkv, qkv, qkv, lq1, lk1, lq2, lk2, subg)


def _pool_kernel(h_ref, halo_ref, g_ref, wg_ref, sc_ref, o_ref, ext_ref, *, ts):
    i = pl.program_id(1)
    x = h_ref[...]
    u = _rms_rows(x, g_ref[...])
    uh = _rms_rows(halo_ref[...], g_ref[...])
    ext_ref[0:POOL_HALO, :] = jnp.where(i > 0, uh, 0.0)
    ext_ref[POOL_HALO:, :] = u
    pos = i * ts + lax.broadcasted_iota(jnp.int32, (ts, 1), 0)
    for g, w in enumerate(POOL_WINDOWS):
        cols = slice(g * POOL_GROUP, (g + 1) * POOL_GROUP)
        win = ext_ref[POOL_HALO:, cols]
        for d in range(1, w):
            win = win + ext_ref[pl.ds(POOL_HALO - d, ts), cols]
        count = jnp.minimum(pos + 1, w).astype(F32)
        pooled = win / count - u[:, cols]
        y = jnp.dot(pooled.astype(BF16), wg_ref[g], preferred_element_type=F32)
        o_ref[:, cols] = x[:, cols] + y * sc_ref[:, cols]


def _pool(h, g, wg, scale, *, batch, seq):
    t, d = h.shape
    ts = min(POOL_ROWS, seq)
    ns = seq // ts
    hb = ts // POOL_HALO
    return pl.pallas_call(
        functools.partial(_pool_kernel, ts=ts),
        grid=(batch, ns),
        in_specs=[
            pl.BlockSpec((ts, d), lambda b, i: (b * ns + i, 0)),
            pl.BlockSpec((POOL_HALO, d), lambda b, i: (jnp.maximum((b * ns + i) * hb - 1, 0), 0)),
            pl.BlockSpec((1, d), lambda b, i: (0, 0)),
            pl.BlockSpec(wg.shape, lambda b, i: (0, 0, 0)),
            pl.BlockSpec((1, d), lambda b, i: (0, 0)),
        ],
        out_specs=pl.BlockSpec((ts, d), lambda b, i: (b * ns + i, 0)),
        out_shape=jax.ShapeDtypeStruct((t, d), F32),
        scratch_shapes=[pltpu.VMEM((ts + POOL_HALO, d), F32)],
        compiler_params=_cparams(("parallel", "arbitrary")),
        name="pool",
    )(h, h, g, wg, scale)


def _masked_heads(x, n_pairs, rows):
    xf = x.astype(F32)
    lane = lax.broadcasted_iota(jnp.int32, (rows, LANES), 1)
    out = []
    for p in range(n_pairs):
        xp = xf[:, p * LANES:(p + 1) * LANES]
        out.append(jnp.where(lane < IDX_DIM, xp, 0.0).astype(BF16))
        out.append(jnp.where(lane >= IDX_DIM, xp, 0.0).astype(BF16))
    return out


def _sum_sublane_groups(m, rows):
    parts = [m[r:r + 8] for r in range(0, rows, 8)]
    while len(parts) > 1:
        parts = [parts[a] + parts[a + 1] for a in range(0, len(parts), 2)]
    return parts[0]


def _dsa_kernel(q_ref, k_ref, v_ref, iq_ref, ik_ref, iw_ref, o_ref, sc_ref, scb_ref, vt_ref, qg_ref, m_ref, acc_ref,
                s0_ref, s1_ref, *, tq, kb, seq, topk, idx_bits, idx_scale):
    i = pl.program_id(1)
    rep = C_HEADS // C_KV_HEADS

    @pl.when(i == 0)
    def _():
        for g in range(C_KV_HEADS):
            _store_vt(vt_ref.at[g], v_ref, slice(g * C_HEAD_DIM, (g + 1) * C_HEAD_DIM), seq, kb)

    nblk = ((i + 1) * tq) // kb
    iw = iw_ref[0] * idx_scale
    iqh = _masked_heads(iq_ref[...], IDX_HEADS // 2, tq)
    qchunk = (i * tq + lax.broadcasted_iota(jnp.int32, (kb, tq), 1)) >> CHUNK_SHIFT
    krow = lax.broadcasted_iota(jnp.int32, (kb, tq), 0)

    def score_block(r):
        ks = pl.multiple_of(r * kb, kb)
        ikb = ik_ref[pl.ds(ks, kb), :]
        acc = jnp.zeros((kb, tq), F32)
        for h in range(IDX_HEADS):
            acc = acc + iw[h:h + 1, :] * jnp.maximum(_dot_nt(ikb, iqh[h]), 0.0)
        acc = jnp.where(((ks + krow) >> CHUNK_SHIFT) <= qchunk, acc, -jnp.inf)
        sc_ref[pl.ds(ks, kb), :] = acc
        scb_ref[pl.ds(ks, kb), :] = acc.astype(BF16)

    def score_pair(t, c):
        score_block(2 * t)
        score_block(2 * t + 1)
        return c

    lax.fori_loop(0, nblk // 2, score_pair, 0)

    @pl.when(nblk % 2 == 1)
    def _():
        score_block(nblk - 1)

    n_acc = 4

    def count(pred):
        def body(r, accs):
            ks = pl.multiple_of(r * kb, kb)
            p = pred(sc_ref[pl.ds(ks, kb), :], ks)
            accs = list(accs)
            for g8 in range(kb // 8):
                a = accs[g8 % n_acc]
                accs[g8 % n_acc] = jnp.where(p[g8 * 8:(g8 + 1) * 8], a + 1.0, a)
            return tuple(accs)
        accs = lax.fori_loop(0, nblk, body, (jnp.zeros((8, tq), F32),) * n_acc)
        return jnp.sum(sum(accs), axis=0, keepdims=True)

    int_min = jnp.int32(-2 ** 31)

    def key_to_f32(key_u):
        ks_ = key_u ^ int_min
        bits = jnp.where(ks_ >= 0, ks_, ks_ ^ jnp.int32(0x7FFFFFFF))
        return lax.bitcast_convert_type(bits, F32)

    one_bf = jnp.ones((), BF16)

    def count_coarse(cand_bf):
        def body(r, accs):
            blk = scb_ref[pl.ds(pl.multiple_of(r * kb, kb), kb), :]
            accs = list(accs)
            for g16 in range(kb // 16):
                a = accs[g16 % n_acc]
                accs[g16 % n_acc] = jnp.where(blk[g16 * 16:(g16 + 1) * 16] >= cand_bf, a + one_bf, a)
            return tuple(accs)
        accs = lax.fori_loop(0, nblk, body, (jnp.zeros((16, tq), BF16),) * n_acc)
        return jnp.sum(sum(a.astype(F32) for a in accs), axis=0, keepdims=True)

    def coarse_body(b, t_u):
        cand_u = t_u | lax.shift_left(jnp.int32(1), 31 - b)
        cnt = count_coarse(key_to_f32(cand_u).astype(BF16))
        return jnp.where(cnt >= topk, cand_u, t_u)

    p_u = lax.fori_loop(0, 16, coarse_body, jnp.zeros((1, tq), jnp.int32))

    lo_u = p_u - jnp.int32(0x10000)

    def fine_body(b, off):
        cand_off = off | lax.shift_left(jnp.int32(1), 16 - b)
        cand = key_to_f32(lo_u + cand_off)
        cnt = count(lambda blk, ks: blk >= cand)
        return jnp.where(cnt >= topk, cand_off, off)

    t_u = lo_u + lax.fori_loop(0, 17, fine_body, jnp.zeros((1, tq), jnp.int32))
    n_adm = (((i * tq + lax.broadcasted_iota(jnp.int32, (1, tq), 1)) >> CHUNK_SHIFT) + 1) << CHUNK_SHIFT
    thr = jnp.where(n_adm < topk, -jnp.inf, key_to_f32(t_u))
    n_ge = count(lambda blk, ks: blk >= thr)

    def tie_cut():
        need = topk - count(lambda blk, ks: blk > thr)

        def idx_body(b, j_u):
            cand = j_u | lax.shift_left(jnp.int32(1), idx_bits - 1 - b)
            cnt = count(lambda blk, ks: (blk == thr) & ((ks + krow) < cand))
            return jnp.where(cnt < need, cand, j_u)

        return lax.fori_loop(0, idx_bits, idx_body, jnp.zeros((1, tq), jnp.int32))

    def no_cut():
        return jnp.full((1, tq), 2 ** idx_bits - 1, jnp.int32)

    jcut = lax.cond(jnp.max(n_ge) > topk, tie_cut, no_cut)
    jcut = jnp.where(thr == -jnp.inf, -1, jcut)

    m_ref[...] = jnp.full(m_ref.shape, NEG, F32)
    acc_ref[...] = jnp.zeros_like(acc_ref)
    for hh in range(C_HEADS):
        qg_ref[hh // rep, (hh % rep) * tq:(hh % rep + 1) * tq, :] = q_ref[:, hh * C_HEAD_DIM:(hh + 1) * C_HEAD_DIM]

    def scores(r, buf):
        ks = pl.multiple_of(r * kb, kb)

        def part(g):
            buf[g] = _dot_nt(k_ref[pl.ds(ks, kb), g * C_HEAD_DIM:(g + 1) * C_HEAD_DIM], qg_ref[g])

        return [functools.partial(part, g) for g in range(C_KV_HEADS)]

    def consume(r, buf, is_last):
        ks = pl.multiple_of(r * kb, kb)
        sc = sc_ref[pl.ds(ks, kb), :]
        sel = (sc > thr) | ((sc == thr) & ((ks + krow) <= jcut))
        bias = jnp.where(sel, 0.0, NEG)
        bias = jnp.concatenate([bias] * rep, axis=1)

        def part(g):
            _softmax_step_t(buf[g] + bias, vt_ref[g, :, pl.ds(ks, kb)], m_ref.at[g], acc_ref.at[g])

        return [functools.partial(part, g) for g in range(C_KV_HEADS)]

    def finish():
        for g in range(C_KV_HEADS):
            a = acc_ref[g]
            ot = a[0:LANES] / a[LANES:LANES + 1]
            for e in range(rep):
                hh = g * rep + e
                o_ref[:, hh * C_HEAD_DIM:(hh + 1) * C_HEAD_DIM] = ot[:, e * tq:(e + 1) * tq].T.astype(BF16)

    _pipelined_tiles(nblk, scores, consume, finish, s0_ref, s1_ref)


def _dsa(proj, iw_t, *, batch, seq, topk):
    t = batch * seq
    tq = min(DSA_TILE, seq)
    nq = seq // tq
    kvw = C_KV_HEADS * C_HEAD_DIM
    iqw = IDX_HEADS * IDX_DIM
    return pl.pallas_call(
        functools.partial(_dsa_kernel, tq=tq, kb=tq, seq=seq, topk=float(topk), idx_bits=(seq - 1).bit_length(),
                          idx_scale=IDX_HEADS ** -0.5 * IDX_DIM ** -0.5),
        grid=(batch, nq),
        in_specs=[
            pl.BlockSpec((tq, C_HEADS * C_HEAD_DIM), lambda b, i: (b * nq + i, 0)),
            pl.BlockSpec((seq, kvw), lambda b, i: (b, C_K0 // kvw)),
            pl.BlockSpec((seq, kvw), lambda b, i: (b, C_V0 // kvw)),
            pl.BlockSpec((tq, iqw), lambda b, i: (b * nq + i, C_IQ0 // iqw)),
            pl.BlockSpec((seq, LANES), lambda b, i: (b, C_IK0 // LANES)),
            pl.BlockSpec((1, 8, tq), lambda b, i: (b, 0, i)),
        ],
        out_specs=pl.BlockSpec((tq, C_HEADS * C_HEAD_DIM), lambda b, i: (b * nq + i, 0)),
        out_shape=jax.ShapeDtypeStruct((t, C_HEADS * C_HEAD_DIM), BF16),
        scratch_shapes=[pltpu.VMEM((seq, tq), F32), pltpu.VMEM((seq, tq), BF16),
                        pltpu.VMEM((C_KV_HEADS, V_ROWS, seq), BF16),
                        pltpu.VMEM((C_KV_HEADS, C_HEADS // C_KV_HEADS * tq, C_HEAD_DIM), BF16),
                        pltpu.VMEM((C_KV_HEADS, 1, C_HEADS // C_KV_HEADS * tq), F32),
                        pltpu.VMEM((C_KV_HEADS, V_ROWS, C_HEADS // C_KV_HEADS * tq), F32)]
                       + [pltpu.VMEM((C_KV_HEADS, tq, C_HEADS // C_KV_HEADS * tq), F32)] * 2,
        compiler_params=_cparams(("arbitrary", "arbitrary")),
        name="dsa",
    )(proj, proj, proj, proj, proj, iw_t)


def _pad_c_weight(w):
    iw = w[:, 2112:2120]
    ik = w[:, 2048:2112]
    pad = jnp.zeros((w.shape[0], C_COLS - C_IW0 - IDX_HEADS), w.dtype)
    return jnp.concatenate([w[:, :2048], ik, ik, iw, pad], axis=1).astype(BF16)


def kernel(x, norm1_g, norm2_g, a_w_in, a_q_norm_g, a_k_norm_g, a_lambda_q1, a_lambda_k1, a_lambda_q2,
           a_lambda_k2, a_subln_g, a_w_out, b_w_group, b_scale, c_w_in, c_q_norm_g, c_k_norm_g, c_w_out,
           mlp_w1, mlp_w2):
    batch, seq, d = x.shape
    depth = norm1_g.shape[0]
    h = x.reshape(batch * seq, d)
    row = lambda v: v.reshape(1, -1).astype(F32)
    for i in range(depth):
        m, j = i % N_MIXERS, i // N_MIXERS
        g1 = row(norm1_g[i])
        mixed = w_out = None
        if m == 0:
            lambda_init = 0.8 - 0.6 * math.exp(-0.3 * i)
            gain = jnp.concatenate([jnp.tile(a_q_norm_g[j], 2 * A_HEADS) * (A_HEAD_DIM ** -0.5 * LOG2E),
                                    jnp.tile(a_k_norm_g[j], 2 * A_HEADS),
                                    jnp.ones((A_HEADS * A_V_DIM,), F32)]).reshape(1, -1)
            qkv = _proj(h, g1, a_w_in[j].astype(BF16), gain, n_norm_cols=2 * D_MODEL, seg=A_HEAD_DIM, aux=False)
            mixed = _attn_a(qkv, row(a_lambda_q1[j]), row(a_lambda_k1[j]), row(a_lambda_q2[j]), row(a_lambda_k2[j]),
                            a_subln_g[j].reshape(-1, 1).astype(F32), batch=batch, seq=seq, lambda_init=lambda_init)
            w_out = a_w_out[j].astype(BF16)
        elif m == 1:
            h = _pool(h, g1, b_w_group[j].astype(BF16), row(b_scale[j]), batch=batch, seq=seq)
        else:
            gain = jnp.concatenate([jnp.tile(c_q_norm_g[j], C_HEADS) * (C_HEAD_DIM ** -0.5 * LOG2E),
                                    jnp.tile(c_k_norm_g[j], C_KV_HEADS),
                                    jnp.ones((C_COLS - C_V0,), F32)]).reshape(1, -1)
            proj, aux = _proj(h, g1, _pad_c_weight(c_w_in[j]), gain, n_norm_cols=C_V0, seg=C_HEAD_DIM, aux=True)
            iw_t = aux[:, :IDX_HEADS].reshape(batch, seq, IDX_HEADS).transpose(0, 2, 1)
            mixed = _dsa(proj, iw_t, batch=batch, seq=seq, topk=min(TOPK_MAX, seq // 4))
            w_out = c_w_out[j].astype(BF16)
        h = _mlp(h, row(norm2_g[i]), mlp_w1, mlp_w2, i, mixed, w_out)
    return h.reshape(batch, seq, d)
```

```python
import functools
import math

import jax
import jax.numpy as jnp
from jax import lax
from jax.experimental import pallas as pl
from jax.experimental.pallas import tpu as pltpu

F32 = jnp.float32
BF16 = jnp.bfloat16

D_MODEL = 1024
N_MIXERS = 3
CHUNK = 64
CHUNK_SHIFT = 6
EPS = 1e-6
LANES = 128
MXU_DIM = 256

A_HEADS = 8
A_HEAD_DIM = 64
A_V_DIM = 128
POOL_WINDOWS = (2, 4, 8, 16)
POOL_GROUP = 256
POOL_HALO = 32
C_HEADS = 8
C_HEAD_DIM = 128
C_KV_HEADS = 2
IDX_HEADS = 8
IDX_DIM = 64
TOPK_MAX = 256

C_Q0, C_K0, C_V0, C_IQ0, C_IK0, C_IW0, C_COLS = 0, 1024, 1280, 1536, 2048, 2176, 2304

NEG = -1e30
LOG2E = math.log2(math.e)
V_ROWS = 128 + 16
VMEM_LIMIT = 56 * 1024 * 1024

PROJ_ROWS = 1024
DENSE_ROWS = 1024
MLP_FF_COLS = 1024
ATTN_A_TILE = 512
POOL_ROWS = 512
DSA_TILE = 256


def _cparams(sem):
    return pltpu.CompilerParams(dimension_semantics=sem, vmem_limit_bytes=VMEM_LIMIT)


def _rms_rows(x, g):
    ms = jnp.mean(x * x, axis=-1, keepdims=True)
    return (x * lax.rsqrt(ms + EPS)) * g


def _dot_nt(a, b):
    return lax.dot_general(a, b, (((1,), (1,)), ((), ())), preferred_element_type=F32)


def _proj_kernel(h_ref, g_ref, w_ref, gain_ref, seg_ref, o_ref, *aux_ref, n_norm_cols, seg):
    u = _rms_rows(h_ref[...], g_ref[...]).astype(BF16)
    n = w_ref.shape[1]
    for c0 in range(0, n, 2 * MXU_DIM):
        cw = min(2 * MXU_DIM, n - c0)
        acc = jnp.dot(u, w_ref[:, c0:c0 + cw], preferred_element_type=F32)
        for s0 in range(0, cw, MXU_DIM):
            cols = slice(c0 + s0, c0 + s0 + MXU_DIM)
            a = acc[:, s0:s0 + MXU_DIM]
            if c0 + s0 < n_norm_cols:
                sq = a * a
                hi = sq.astype(BF16)
                lo = (sq - hi.astype(F32)).astype(BF16)
                ssum = (jnp.dot(hi, seg_ref[...], preferred_element_type=F32)
                        + jnp.dot(lo, seg_ref[...], preferred_element_type=F32))
                a = (a * lax.rsqrt(ssum * (1.0 / seg) + EPS)) * gain_ref[:, cols]
            o_ref[:, cols] = a.astype(BF16)
        if aux_ref and c0 + cw == n:
            aux_ref[0][...] = acc[:, cw - LANES:]


def _proj(h, g, w, gain, *, n_norm_cols, seg, aux):
    t, d = h.shape
    n = w.shape[1]
    tm = min(PROJ_ROWS, t)
    assert n % MXU_DIM == 0 and n_norm_cols % MXU_DIM == 0 and t % tm == 0
    r = lax.broadcasted_iota(jnp.int32, (MXU_DIM, MXU_DIM), 0) // seg
    c = lax.broadcasted_iota(jnp.int32, (MXU_DIM, MXU_DIM), 1) // seg
    segm = (r == c).astype(BF16)
    out_shape = [jax.ShapeDtypeStruct((t, n), BF16)]
    out_specs = [pl.BlockSpec((tm, n), lambda i: (i, 0))]
    if aux:
        out_shape.append(jax.ShapeDtypeStruct((t, LANES), F32))
        out_specs.append(pl.BlockSpec((tm, LANES), lambda i: (i, 0)))
    res = pl.pallas_call(
        functools.partial(_proj_kernel, n_norm_cols=n_norm_cols, seg=seg),
        grid=(t // tm,),
        in_specs=[
            pl.BlockSpec((tm, d), lambda i: (i, 0)),
            pl.BlockSpec((1, d), lambda i: (0, 0)),
            pl.BlockSpec((d, n), lambda i: (0, 0)),
            pl.BlockSpec((1, n), lambda i: (0, 0)),
            pl.BlockSpec((MXU_DIM, MXU_DIM), lambda i: (0, 0)),
        ],
        out_specs=out_specs,
        out_shape=out_shape,
        compiler_params=_cparams(("parallel",)),
        name="proj",
    )(h, g, w, gain, segm)
    return res if aux else res[0]


def _mlp_kernel(h_ref, g_ref, w1_ref, w2_ref, *rest):
    *mix, o_ref, u_ref = rest
    j = pl.program_id(1)

    @pl.when(j == 0)
    def _():
        hn = h_ref[...]
        if mix:
            hn = hn + jnp.dot(mix[0][...], mix[1][...], preferred_element_type=F32)
        o_ref[...] = hn
        u_ref[...] = _rms_rows(hn, g_ref[...]).astype(BF16)

    hid = jnp.dot(u_ref[...], w1_ref[...].astype(BF16), preferred_element_type=F32)
    hid = jnp.square(jnp.maximum(hid, 0.0)).astype(BF16)
    o_ref[...] += jnp.dot(hid, w2_ref[...].astype(BF16), preferred_element_type=F32)


def _mlp(h, g, w1, w2, layer, mixed=None, w_out=None):
    t, d = h.shape
    f = w1.shape[2]
    tm = min(DENSE_ROWS, t)
    tf = min(MLP_FF_COLS, f)
    operands = [h, g, w1, w2]
    in_specs = [
        pl.BlockSpec((tm, d), lambda i, j: (i, 0)),
        pl.BlockSpec((1, d), lambda i, j: (0, 0)),
        pl.BlockSpec((None, d, tf), lambda i, j: (layer, 0, j)),
        pl.BlockSpec((None, tf, d), lambda i, j: (layer, j, 0)),
    ]
    if mixed is not None:
        operands += [mixed, w_out]
        in_specs += [pl.BlockSpec((tm, mixed.shape[1]), lambda i, j: (i, 0)),
                     pl.BlockSpec(w_out.shape, lambda i, j: (0, 0))]
    return pl.pallas_call(
        _mlp_kernel,
        grid=(t // tm, f // tf),
        in_specs=in_specs,
        out_specs=pl.BlockSpec((tm, d), lambda i, j: (i, 0)),
        out_shape=jax.ShapeDtypeStruct((t, d), F32),
        scratch_shapes=[pltpu.VMEM((tm, d), BF16)],
        compiler_params=_cparams(("parallel", "arbitrary")),
        name="mlp",
    )(*operands)


def _softmax_step_t(st, vt1, m_ref, acc_ref):
    m_prev = m_ref[...]
    m_next = jnp.maximum(m_prev, jnp.max(st, axis=0, keepdims=True))
    p = jnp.exp2(st - m_next).astype(BF16)
    alpha = jnp.exp2(m_prev - m_next)
    m_ref[...] = m_next
    acc_ref[...] = alpha * acc_ref[...] + jnp.dot(vt1, p, preferred_element_type=F32)


def _pipelined_tiles(n, scores, consume, finish, buf_a, buf_b):
    n_pairs = (n - 1) // 2

    def run(parts):
        for part in parts:
            part()

    def both(j_scores, buf_scores, j_consume, buf_consume):
        for s_part, c_part in zip(scores(j_scores, buf_scores), consume(j_consume, buf_consume, False)):
            s_part()
            c_part()

    run(scores(0, buf_a))

    def pair(t, c):
        both(2 * t + 1, buf_b, 2 * t, buf_a)
        both(2 * t + 2, buf_a, 2 * t + 1, buf_b)
        return c

    lax.fori_loop(0, n_pairs, pair, 0)

    @pl.when(n - 2 * n_pairs == 1)
    def _():
        run(consume(n - 1, buf_a, True))
        finish()

    @pl.when(n - 2 * n_pairs == 2)
    def _():
        both(n - 1, buf_b, n - 2, buf_a)
        run(consume(n - 1, buf_b, True))
        finish()


def _store_vt(vt_ref, v_ref, cols, seq, blk):
    for c in range(seq // blk):
        rows = slice(c * blk, (c + 1) * blk)
        vt_ref[0:LANES, rows] = v_ref[rows, cols].astype(F32).T.astype(BF16)
    vt_ref[LANES:, :] = jnp.ones((V_ROWS - LANES, seq), BF16)


A_HEADS_PER_STEP = 4


def _attn_a_kernel(q_ref, k_ref, v_ref, lq1_ref, lk1_ref, lq2_ref, lk2_ref, subg_ref, o_ref,
                   vt_ref, qm_ref, m_ref, acc_ref, s0_ref, s1_ref, *, tq, tk, seq, lambda_init):
    i = pl.program_id(2)
    heads = range(A_HEADS_PER_STEP)
    hcols = lambda hh: slice(hh * LANES, (hh + 1) * LANES)

    @pl.when(i == 0)
    def _():
        for hh in heads:
            _store_vt(vt_ref.at[hh], v_ref, hcols(hh), seq, tk)

    lane = lax.broadcasted_iota(jnp.int32, (tq, LANES), 1)
    for hh in heads:
        q = q_ref[:, hcols(hh)].astype(F32)
        qm_ref[hh, 0:tq, :] = jnp.where(lane < A_HEAD_DIM, q, 0.0).astype(BF16)
        qm_ref[hh, tq:, :] = jnp.where(lane >= A_HEAD_DIM, q, 0.0).astype(BF16)
    m_ref[...] = jnp.full(m_ref.shape, NEG, F32)
    acc_ref[...] = jnp.zeros_like(acc_ref)

    parts = [(hh, slice(mp * tq, (mp + 1) * tq)) for hh in heads for mp in range(2)]

    def scores(j, buf):
        ks = pl.multiple_of(j * tk, tk)

        def part(hh, lanes):
            buf[hh, :, lanes] = _dot_nt(k_ref[pl.ds(ks, tk), hcols(hh)], qm_ref[hh, lanes, :])

        return [functools.partial(part, hh, lanes) for hh, lanes in parts]

    def consume(j, buf, masked):
        ks = pl.multiple_of(j * tk, tk)
        if masked:
            kpos = ks + lax.broadcasted_iota(jnp.int32, (tk, tq), 0)
            qpos = i * tq + lax.broadcasted_iota(jnp.int32, (tk, tq), 1)
            ok = (kpos >> CHUNK_SHIFT) <= (qpos >> CHUNK_SHIFT)

        def part(hh, lanes):
            st = buf[hh, :, lanes]
            if masked:
                st = jnp.where(ok, st, NEG)
            _softmax_step_t(st, vt_ref[hh, :, pl.ds(ks, tk)], m_ref.at[hh, :, lanes], acc_ref.at[hh, :, lanes])

        return [functools.partial(part, hh, lanes) for hh, lanes in parts]

    def finish():
        lam = (jnp.exp(jnp.sum(lq1_ref[...] * lk1_ref[...], axis=1, keepdims=True))
               - jnp.exp(jnp.sum(lq2_ref[...] * lk2_ref[...], axis=1, keepdims=True)) + lambda_init)
        for hh in heads:
            acc = acc_ref[hh]
            ot = acc[0:LANES] / acc[LANES:LANES + 1]
            od = ot[:, :tq] - lam * ot[:, tq:]
            ms = jnp.mean(od * od, axis=0, keepdims=True)
            y = (od * lax.rsqrt(ms + EPS)) * subg_ref[...] * (1.0 - lambda_init)
            o_ref[:, hcols(hh)] = y.T.astype(BF16)

    _pipelined_tiles(((i + 1) * tq + tk - 1) // tk, scores, consume, finish, s0_ref, s1_ref)


def _attn_a(qkv, lq1, lk1, lq2, lk2, subg, *, batch, seq, lambda_init):
    t = batch * seq
    tq = tk = min(ATTN_A_TILE, seq)
    nq = seq // tq
    hp = A_HEADS_PER_STEP
    ngrp = A_HEADS // hp
    vec = lambda n: pl.BlockSpec((1, n), lambda b, h, i: (0, 0))
    return pl.pallas_call(
        functools.partial(_attn_a_kernel, tq=tq, tk=tk, seq=seq, lambda_init=lambda_init),
        grid=(batch, ngrp, nq),
        in_specs=[
            pl.BlockSpec((tq, hp * LANES), lambda b, h, i: (b * nq + i, h)),
            pl.BlockSpec((seq, hp * LANES), lambda b, h, i: (b, ngrp + h)),
            pl.BlockSpec((seq, hp * LANES), lambda b, h, i: (b, 2 * ngrp + h)),
            vec(A_HEAD_DIM), vec(A_HEAD_DIM), vec(A_HEAD_DIM), vec(A_HEAD_DIM),
            pl.BlockSpec((A_V_DIM, 1), lambda b, h, i: (0, 0)),
        ],
        out_specs=pl.BlockSpec((tq, hp * LANES), lambda b, h, i: (b * nq + i, h)),
        out_shape=jax.ShapeDtypeStruct((t, A_HEADS * A_V_DIM), BF16),
        scratch_shapes=[pltpu.VMEM((hp, V_ROWS, seq), BF16), pltpu.VMEM((hp, 2 * tq, LANES), BF16),
                        pltpu.VMEM((hp, 1, 2 * tq), F32),
                        pltpu.VMEM((hp, V_ROWS, 2 * tq), F32)] + [pltpu.VMEM((hp, tk, 2 * tq), F32)] * 2,
        compiler_params=_cparams(("arbitrary", "arbitrary", "arbitrary")),
        name="attn_a",
    )(qkv, qkv, qkv, lq1, lk1, lq2, lk2, subg)


def _pool_kernel(h_ref, halo_ref, g_ref, wg_ref, sc_ref, o_ref, ext_ref, la_ref, lb_ref, *, ts):
    i = pl.program_id(1)
    x = h_ref[...]
    u = _rms_rows(x, g_ref[...])
    uh = _rms_rows(halo_ref[...], g_ref[...])
    ext_ref[0:POOL_HALO, :] = jnp.where(i > 0, uh, 0.0)
    ext_ref[POOL_HALO:, :] = u
    pos = i * ts + lax.broadcasted_iota(jnp.int32, (ts, 1), 0)
    for g, w in enumerate(POOL_WINDOWS):
        cols = slice(g * POOL_GROUP, (g + 1) * POOL_GROUP)
        src, src_cols, sh, lo, k = ext_ref, cols, 1, 8, 0
        while sh < w:
            n = ts + POOL_HALO - lo
            win = src[pl.ds(lo, n), src_cols] + src[pl.ds(lo - sh, n), src_cols]
            if 2 * sh < w:
                dst = (la_ref, lb_ref)[k % 2]
                dst[pl.ds(lo, n), :] = win
                src, src_cols = dst, slice(None)
            sh, lo, k = 2 * sh, lo + 8, k + 1
        win = win[POOL_HALO - (lo - 8):]
        count = jnp.minimum(pos + 1, w).astype(F32)
        pooled = win / count - u[:, cols]
        y = jnp.dot(pooled.astype(BF16), wg_ref[g], preferred_element_type=F32)
        o_ref[:, cols] = x[:, cols] + y * sc_ref[:, cols]


def _pool(h, g, wg, scale, *, batch, seq):
    t, d = h.shape
    ts = min(POOL_ROWS, seq)
    ns = seq // ts
    hb = ts // POOL_HALO
    return pl.pallas_call(
        functools.partial(_pool_kernel, ts=ts),
        grid=(batch, ns),
        in_specs=[
            pl.BlockSpec((ts, d), lambda b, i: (b * ns + i, 0)),
            pl.BlockSpec((POOL_HALO, d), lambda b, i: (jnp.maximum((b * ns + i) * hb - 1, 0), 0)),
            pl.BlockSpec((1, d), lambda b, i: (0, 0)),
            pl.BlockSpec(wg.shape, lambda b, i: (0, 0, 0)),
            pl.BlockSpec((1, d), lambda b, i: (0, 0)),
        ],
        out_specs=pl.BlockSpec((ts, d), lambda b, i: (b * ns + i, 0)),
        out_shape=jax.ShapeDtypeStruct((t, d), F32),
        scratch_shapes=[pltpu.VMEM((ts + POOL_HALO, d), F32)] + [pltpu.VMEM((ts + POOL_HALO, POOL_GROUP), F32)] * 2,
        compiler_params=_cparams(("parallel", "arbitrary")),
        name="pool",
    )(h, h, g, wg, scale)


def _masked_heads(x, n_pairs, rows):
    xf = x.astype(F32)
    lane = lax.broadcasted_iota(jnp.int32, (rows, LANES), 1)
    out = []
    for p in range(n_pairs):
        xp = xf[:, p * LANES:(p + 1) * LANES]
        out.append(jnp.where(lane < IDX_DIM, xp, 0.0).astype(BF16))
        out.append(jnp.where(lane >= IDX_DIM, xp, 0.0).astype(BF16))
    return out


def _sum_sublane_groups(m, rows):
    parts = [m[r:r + 8] for r in range(0, rows, 8)]
    while len(parts) > 1:
        parts = [parts[a] + parts[a + 1] for a in range(0, len(parts), 2)]
    return parts[0]


def _dsa_kernel(q_ref, k_ref, v_ref, iq_ref, ik_ref, iw_ref, o_ref, sc_ref, scb_ref, vt_ref, qg_ref, m_ref, acc_ref,
                s0_ref, s1_ref, *, tq, kb, seq, topk, idx_bits, idx_scale):
    i = pl.program_id(1)
    rep = C_HEADS // C_KV_HEADS

    @pl.when(i == 0)
    def _():
        for g in range(C_KV_HEADS):
            _store_vt(vt_ref.at[g], v_ref, slice(g * C_HEAD_DIM, (g + 1) * C_HEAD_DIM), seq, kb)

    nblk = ((i + 1) * tq) // kb
    iw = iw_ref[0] * idx_scale
    iqh = _masked_heads(iq_ref[...], IDX_HEADS // 2, tq)
    qchunk = (i * tq + lax.broadcasted_iota(jnp.int32, (kb, tq), 1)) >> CHUNK_SHIFT
    krow = lax.broadcasted_iota(jnp.int32, (kb, tq), 0)

    def score_block(r):
        ks = pl.multiple_of(r * kb, kb)
        ikb = ik_ref[pl.ds(ks, kb), :]
        acc = jnp.zeros((kb, tq), F32)
        for h in range(IDX_HEADS):
            acc = acc + iw[h:h + 1, :] * jnp.maximum(_dot_nt(ikb, iqh[h]), 0.0)
        acc = jnp.where(((ks + krow) >> CHUNK_SHIFT) <= qchunk, acc, -jnp.inf)
        sc_ref[pl.ds(ks, kb), :] = acc
        scb_ref[pl.ds(ks, kb), :] = acc.astype(BF16)

    def score_pair(t, c):
        score_block(2 * t)
        score_block(2 * t + 1)
        return c

    lax.fori_loop(0, nblk // 2, score_pair, 0)

    @pl.when(nblk % 2 == 1)
    def _():
        score_block(nblk - 1)

    n_acc = 4

    def count(pred):
        def body(r, accs):
            ks = pl.multiple_of(r * kb, kb)
            p = pred(sc_ref[pl.ds(ks, kb), :], ks)
            accs = list(accs)
            for g8 in range(kb // 8):
                a = accs[g8 % n_acc]
                accs[g8 % n_acc] = jnp.where(p[g8 * 8:(g8 + 1) * 8], a + 1.0, a)
            return tuple(accs)
        accs = lax.fori_loop(0, nblk, body, (jnp.zeros((8, tq), F32),) * n_acc)
        return jnp.sum(sum(accs), axis=0, keepdims=True)

    int_min = jnp.int32(-2 ** 31)

    def key_to_f32(key_u):
        ks_ = key_u ^ int_min
        bits = jnp.where(ks_ >= 0, ks_, ks_ ^ jnp.int32(0x7FFFFFFF))
        return lax.bitcast_convert_type(bits, F32)

    one_bf = jnp.ones((), BF16)

    def count_coarse(cand_bf):
        def body(r, accs):
            blk = scb_ref[pl.ds(pl.multiple_of(r * kb, kb), kb), :]
            accs = list(accs)
            for g16 in range(kb // 16):
                a = accs[g16 % n_acc]
                accs[g16 % n_acc] = jnp.where(blk[g16 * 16:(g16 + 1) * 16] >= cand_bf, a + one_bf, a)
            return tuple(accs)
        accs = lax.fori_loop(0, nblk, body, (jnp.zeros((16, tq), BF16),) * n_acc)
        return jnp.sum(sum(a.astype(F32) for a in accs), axis=0, keepdims=True)

    def coarse_body(b, t_u):
        cand_u = t_u | lax.shift_left(jnp.int32(1), 31 - b)
        cnt = count_coarse(key_to_f32(cand_u).astype(BF16))
        return jnp.where(cnt >= topk, cand_u, t_u)

    p_u = lax.fori_loop(0, 16, coarse_body, jnp.zeros((1, tq), jnp.int32))

    lo_u = p_u - jnp.int32(0x10000)

    def fine_body(b, off):
        cand_off = off | lax.shift_left(jnp.int32(1), 16 - b)
        cand = key_to_f32(lo_u + cand_off)
        cnt = count(lambda blk, ks: blk >= cand)
        return jnp.where(cnt >= topk, cand_off, off)

    t_u = lo_u + lax.fori_loop(0, 17, fine_body, jnp.zeros((1, tq), jnp.int32))
    n_adm = (((i * tq + lax.broadcasted_iota(jnp.int32, (1, tq), 1)) >> CHUNK_SHIFT) + 1) << CHUNK_SHIFT
    thr = jnp.where(n_adm < topk, -jnp.inf, key_to_f32(t_u))
    n_ge = count(lambda blk, ks: blk >= thr)

    def tie_cut():
        need = topk - count(lambda blk, ks: blk > thr)

        def idx_body(b, j_u):
            cand = j_u | lax.shift_left(jnp.int32(1), idx_bits - 1 - b)
            cnt = count(lambda blk, ks: (blk == thr) & ((ks + krow) < cand))
            return jnp.where(cnt < need, cand, j_u)

        return lax.fori_loop(0, idx_bits, idx_body, jnp.zeros((1, tq), jnp.int32))

    def no_cut():
        return jnp.full((1, tq), 2 ** idx_bits - 1, jnp.int32)

    jcut = lax.cond(jnp.max(n_ge) > topk, tie_cut, no_cut)
    jcut = jnp.where(thr == -jnp.inf, -1, jcut)

    m_ref[...] = jnp.full(m_ref.shape, NEG, F32)
    acc_ref[...] = jnp.zeros_like(acc_ref)
    for hh in range(C_HEADS):
        qg_ref[hh // rep, (hh % rep) * tq:(hh % rep + 1) * tq, :] = q_ref[:, hh * C_HEAD_DIM:(hh + 1) * C_HEAD_DIM]

    def scores(r, buf):
        ks = pl.multiple_of(r * kb, kb)

        def part(g):
            buf[g] = _dot_nt(k_ref[pl.ds(ks, kb), g * C_HEAD_DIM:(g + 1) * C_HEAD_DIM], qg_ref[g])

        return [functools.partial(part, g) for g in range(C_KV_HEADS)]

    def consume(r, buf, is_last):
        ks = pl.multiple_of(r * kb, kb)
        sc = sc_ref[pl.ds(ks, kb), :]
        sel = (sc > thr) | ((sc == thr) & ((ks + krow) <= jcut))
        bias = jnp.where(sel, 0.0, NEG)
        bias = jnp.concatenate([bias] * rep, axis=1)

        def part(g):
            _softmax_step_t(buf[g] + bias, vt_ref[g, :, pl.ds(ks, kb)], m_ref.at[g], acc_ref.at[g])

        return [functools.partial(part, g) for g in range(C_KV_HEADS)]

    def finish():
        for g in range(C_KV_HEADS):
            a = acc_ref[g]
            ot = a[0:LANES] / a[LANES:LANES + 1]
            for e in range(rep):
                hh = g * rep + e
                o_ref[:, hh * C_HEAD_DIM:(hh + 1) * C_HEAD_DIM] = ot[:, e * tq:(e + 1) * tq].T.astype(BF16)

    _pipelined_tiles(nblk, scores, consume, finish, s0_ref, s1_ref)


def _dsa(proj, iw_t, *, batch, seq, topk):
    t = batch * seq
    tq = min(DSA_TILE, seq)
    nq = seq // tq
    kvw = C_KV_HEADS * C_HEAD_DIM
    iqw = IDX_HEADS * IDX_DIM
    return pl.pallas_call(
        functools.partial(_dsa_kernel, tq=tq, kb=tq, seq=seq, topk=float(topk), idx_bits=(seq - 1).bit_length(),
                          idx_scale=IDX_HEADS ** -0.5 * IDX_DIM ** -0.5),
        grid=(batch, nq),
        in_specs=[
            pl.BlockSpec((tq, C_HEADS * C_HEAD_DIM), lambda b, i: (b * nq + i, 0)),
            pl.BlockSpec((seq, kvw), lambda b, i: (b, C_K0 // kvw)),
            pl.BlockSpec((seq, kvw), lambda b, i: (b, C_V0 // kvw)),
            pl.BlockSpec((tq, iqw), lambda b, i: (b * nq + i, C_IQ0 // iqw)),
            pl.BlockSpec((seq, LANES), lambda b, i: (b, C_IK0 // LANES)),
            pl.BlockSpec((1, 8, tq), lambda b, i: (b, 0, i)),
        ],
        out_specs=pl.BlockSpec((tq, C_HEADS * C_HEAD_DIM), lambda b, i: (b * nq + i, 0)),
        out_shape=jax.ShapeDtypeStruct((t, C_HEADS * C_HEAD_DIM), BF16),
        scratch_shapes=[pltpu.VMEM((seq, tq), F32), pltpu.VMEM((seq, tq), BF16),
                        pltpu.VMEM((C_KV_HEADS, V_ROWS, seq), BF16),
                        pltpu.VMEM((C_KV_HEADS, C_HEADS // C_KV_HEADS * tq, C_HEAD_DIM), BF16),
                        pltpu.VMEM((C_KV_HEADS, 1, C_HEADS // C_KV_HEADS * tq), F32),
                        pltpu.VMEM((C_KV_HEADS, V_ROWS, C_HEADS // C_KV_HEADS * tq), F32)]
                       + [pltpu.VMEM((C_KV_HEADS, tq, C_HEADS // C_KV_HEADS * tq), F32)] * 2,
        compiler_params=_cparams(("arbitrary", "arbitrary")),
        name="dsa",
    )(proj, proj, proj, proj, proj, iw_t)


def _pad_c_weight(w):
    iw = w[:, 2112:2120]
    ik = w[:, 2048:2112]
    pad = jnp.zeros((w.shape[0], C_COLS - C_IW0 - IDX_HEADS), w.dtype)
    return jnp.concatenate([w[:, :2048], ik, ik, iw, pad], axis=1).astype(BF16)


def kernel(x, norm1_g, norm2_g, a_w_in, a_q_norm_g, a_k_norm_g, a_lambda_q1, a_lambda_k1, a_lambda_q2,
           a_lambda_k2, a_subln_g, a_w_out, b_w_group, b_scale, c_w_in, c_q_norm_g, c_k_norm_g, c_w_out,
           mlp_w1, mlp_w2):
    batch, seq, d = x.shape
    depth = norm1_g.shape[0]
    h = x.reshape(batch * seq, d)
    row = lambda v: v.reshape(1, -1).astype(F32)
    for i in range(depth):
        m, j = i % N_MIXERS, i // N_MIXERS
        g1 = row(norm1_g[i])
        mixed = w_out = None
        if m == 0:
            lambda_init = 0.8 - 0.6 * math.exp(-0.3 * i)
            gain = jnp.concatenate([jnp.tile(a_q_norm_g[j], 2 * A_HEADS) * (A_HEAD_DIM ** -0.5 * LOG2E),
                                    jnp.tile(a_k_norm_g[j], 2 * A_HEADS),
                                    jnp.ones((A_HEADS * A_V_DIM,), F32)]).reshape(1, -1)
            qkv = _proj(h, g1, a_w_in[j].astype(BF16), gain, n_norm_cols=2 * D_MODEL, seg=A_HEAD_DIM, aux=False)
            mixed = _attn_a(qkv, row(a_lambda_q1[j]), row(a_lambda_k1[j]), row(a_lambda_q2[j]), row(a_lambda_k2[j]),
                            a_subln_g[j].reshape(-1, 1).astype(F32), batch=batch, seq=seq, lambda_init=lambda_init)
            w_out = a_w_out[j].astype(BF16)
        elif m == 1:
            h = _pool(h, g1, b_w_group[j].astype(BF16), row(b_scale[j]), batch=batch, seq=seq)
        else:
            gain = jnp.concatenate([jnp.tile(c_q_norm_g[j], C_HEADS) * (C_HEAD_DIM ** -0.5 * LOG2E),
                                    jnp.tile(c_k_norm_g[j], C_KV_HEADS),
                                    jnp.ones((C_COLS - C_V0,), F32)]).reshape(1, -1)
            proj, aux = _proj(h, g1, _pad_c_weight(c_w_in[j]), gain, n_norm_cols=C_V0, seg=C_HEAD_DIM, aux=True)
            iw_t = aux[:, :IDX_HEADS].reshape(batch, seq, IDX_HEADS).transpose(0, 2, 1)
            mixed = _dsa(proj, iw_t, batch=batch, seq=seq, topk=min(TOPK_MAX, seq // 4))
            w_out = c_w_out[j].astype(BF16)
        h = _mlp(h, row(norm2_g[i]), mlp_w1, mlp_w2, i, mixed, w_out)
    return h.reshape(batch, seq, d)
```
